```python
import jax, jax.numpy as jnp
from jax import lax
import numpy as np

D_MODEL = 2048
BATCH = 4
SEQ = 2048
DEPTH = 2
DEC_BATCH = 128
DEC_SEQ = 1
PAST_LEN = 16384
PAGE_SIZE = 128

D_MIX = D_MODEL
HG_WIDTH = D_MIX // 2
RG_WIDTH = D_MIX // 4
PL_WIDTH = D_MIX - HG_WIDTH - RG_WIDTH
HG_HEAD_DIM = 128
HG_HEADS = HG_WIDTH // HG_HEAD_DIM
HG_CHUNK = 32
RG_BLOCKS = 8
RG_BLOCK_DIM = RG_WIDTH // RG_BLOCKS
RG_CONV = 4
RG_C = 8.0
POOL_WINDOWS = (2, 4, 8, 16)
PL_GROUPS = len(POOL_WINDOWS)
PL_GROUP_DIM = PL_WIDTH // PL_GROUPS
POOL_BUF = max(POOL_WINDOWS) - 1
N_EXPERTS = 32
TOP_K = 4
D_FF = D_MODEL
SWIGLU_LIMIT = 7.0
SWIGLU_ALPHA = 1.702
LN_EPS = 1e-5
RMS_EPS = 1e-6
DN_ALPHA = (2 * DEPTH) ** 0.25
DN_BETA = (8 * DEPTH) ** -0.25
D_IN = 4 * HG_WIDTH + 2 * RG_WIDTH + PL_WIDTH
SPLITS = [HG_WIDTH, 2 * HG_WIDTH, 3 * HG_WIDTH, 4 * HG_WIDTH, 4 * HG_WIDTH + RG_WIDTH, 4 * HG_WIDTH + 2 * RG_WIDTH]

kernel_name = "hymba_hgrn2_rglru_pool_moe_step"


def layer_norm(x, g, b):
    xf = x.astype(jnp.float32)
    mu = jnp.mean(xf, axis=-1, keepdims=True)
    xc = xf - mu
    var = jnp.mean(xc * xc, axis=-1, keepdims=True)
    return (xc * lax.rsqrt(var + LN_EPS) * g.astype(jnp.float32) + b.astype(jnp.float32)).astype(x.dtype)


def hgrn_lower_bounds(lb_logits):
    p = jax.nn.softmax(lb_logits.astype(jnp.float32), axis=0)
    c = jnp.cumsum(p, axis=0)
    return c - c[0]


def _hgrn_chunk_step(S, inp):
    q, k, v, logf = inp
    C = q.shape[2]
    b = jnp.cumsum(logf, axis=2)
    causal = jnp.tril(jnp.ones((C, C), dtype=bool))
    diff = b[:, :, :, None, :] - b[:, :, None, :, :]
    decay = jnp.exp(jnp.where(causal[None, None, :, :, None], diff, -jnp.inf))
    scores = jnp.einsum('bhtk,bhsk,bhtsk->bhts', q, k, decay)
    o = jnp.einsum('bhts,bhsv->bhtv', scores, v) + jnp.einsum('bhtk,bhkv->bhtv', q * jnp.exp(b), S)
    b_last = b[:, :, -1:, :]
    S_new = jnp.exp(b_last[:, :, 0, :])[..., None] * S + jnp.einsum('bhsk,bhsv->bhkv', k * jnp.exp(b_last - b), v)
    return S_new, o


def hgrn2_mixer(hq, hf, hi, hg, lb, norm_w, S0):
    Bt, L, _ = hq.shape
    dt = hq.dtype
    z = hf.astype(jnp.float32)
    q = jax.nn.silu(hq.astype(jnp.float32))
    logf = jnp.logaddexp(jnp.log(lb), jnp.log1p(-lb) + jax.nn.log_sigmoid(z))
    k = (1.0 - lb) * jax.nn.sigmoid(-z)
    v = hi.astype(jnp.float32)

    def heads(t):
        return t.reshape(Bt, L, HG_HEADS, HG_HEAD_DIM).transpose(0, 2, 1, 3)

    q, k, v, logf = heads(q), heads(k), heads(v), heads(logf)
    C = min(HG_CHUNK, L)
    nC = -(-L // C)
    pad = nC * C - L

    def chunks(t):
        t = jnp.pad(t, ((0, 0), (0, 0), (0, pad), (0, 0)))
        return jnp.moveaxis(t.reshape(Bt, HG_HEADS, nC, C, t.shape[-1]), 2, 0)

    S_fin, o = lax.scan(_hgrn_chunk_step, S0.astype(jnp.float32), (chunks(q), chunks(k), chunks(v), chunks(logf)))
    o = jnp.moveaxis(o, 0, 2).reshape(Bt, HG_HEADS, nC * C, HG_HEAD_DIM)[:, :, :L]
    o = o.transpose(0, 2, 1, 3)
    g = hg.astype(jnp.float32).reshape(Bt, L, HG_HEADS, HG_HEAD_DIM)
    o = o * lax.rsqrt(jnp.mean(o * o, axis=-1, keepdims=True) + RMS_EPS) * norm_w.astype(jnp.float32) * jax.nn.silu(g)
    return o.reshape(Bt, L, HG_WIDTH).astype(dt), S_fin


def rglru_mixer(hx, hg, conv_buf, h0, conv_w, conv_b, wa, ba, wx, bx, lam):
    Bt, L, W = hx.shape
    dt = hx.dtype
    x_ext = jnp.concatenate([conv_buf.astype(dt), hx], axis=1)
    xc = sum(x_ext[:, j:j + L] * conv_w[j] for j in range(RG_CONV)) + conv_b
    new_buf = x_ext[:, -(RG_CONV - 1):]
    xf = xc.astype(jnp.float32)
    xb = xf.reshape(Bt, L, RG_BLOCKS, RG_BLOCK_DIM)
    r = jax.nn.sigmoid(jnp.einsum('blhi,hij->blhj', xb, wa.astype(jnp.float32)).reshape(Bt, L, W) + ba)
    i = jax.nn.sigmoid(jnp.einsum('blhi,hij->blhj', xb, wx.astype(jnp.float32)).reshape(Bt, L, W) + bx)
    log_a = RG_C * r * jax.nn.log_sigmoid(lam.astype(jnp.float32))
    a = jnp.exp(log_a)
    bterm = jnp.sqrt(-jnp.expm1(2.0 * log_a)) * (i * xf)
    bterm = bterm.at[:, 0].add(a[:, 0] * h0.astype(jnp.float32))

    def comb(c1, c2):
        a1, b1 = c1
        a2, b2 = c2
        return a1 * a2, a2 * b1 + b2

    _, h = lax.associative_scan(comb, (a, bterm), axis=1)
    y = h * jax.nn.gelu(hg.astype(jnp.float32))
    return y.astype(dt), new_buf, h[:, -1]


def pool_mixer(hp, buf, start_pos, pool_w, pool_scale):
    Bt, L, W = hp.shape
    dt = hp.dtype
    x_ext = jnp.concatenate([buf.astype(dt), hp], axis=1)
    xe = x_ext.astype(jnp.float32)
    cs = jnp.concatenate([jnp.zeros((Bt, 1, W), jnp.float32), jnp.cumsum(xe, axis=1)], axis=1)
    pos = start_pos + jnp.arange(L)
    outs = []
    for g, w in enumerate(POOL_WINDOWS):
        sl = slice(g * PL_GROUP_DIM, (g + 1) * PL_GROUP_DIM)
        win_sum = cs[:, POOL_BUF + 1:POOL_BUF + 1 + L, sl] - cs[:, POOL_BUF + 1 - w:POOL_BUF + 1 - w + L, sl]
        count = jnp.minimum(pos + 1, w).astype(jnp.float32)
        outs.append(win_sum / count[None, :, None] - xe[:, POOL_BUF:, sl])
    pooled = jnp.stack(outs, axis=2)
    y = jnp.einsum('blgc,gcd->blgd', pooled, pool_w.astype(jnp.float32)).reshape(Bt, L, W) * pool_scale.astype(jnp.float32)
    return y.astype(dt), x_ext[:, -POOL_BUF:]


def token_mixers(x, S0, conv_buf, h0, pool_buf, start_pos, w_in, lb, hg_norm_w, rg_conv_w, rg_conv_b,
                 rg_wa, rg_ba, rg_wx, rg_bx, rg_lambda, pool_w, pool_scale):
    h = x @ w_in
    hq, hf, hi, hg, rx, rgate, pin = jnp.split(h, SPLITS, axis=-1)
    o_hg, S_new = hgrn2_mixer(hq, hf, hi, hg, lb, hg_norm_w, S0)
    o_rg, cbuf_new, h_new = rglru_mixer(rx, rgate, conv_buf, h0, rg_conv_w, rg_conv_b, rg_wa, rg_ba, rg_wx, rg_bx, rg_lambda)
    o_pl, pbuf_new = pool_mixer(pin, pool_buf, start_pos, pool_w, pool_scale)
    return jnp.concatenate([o_hg, o_rg, o_pl], axis=-1), S_new, h_new, cbuf_new, pbuf_new


def moe(x, rw, rb, wgu, bgu, wd, bd):
    logits = (x @ rw + rb).astype(jnp.float32)
    top_v, top_i = lax.top_k(logits, TOP_K)
    gates = jax.nn.softmax(top_v, axis=-1)
    comb = jnp.einsum('mk,mke->me', gates, jax.nn.one_hot(top_i, N_EXPERTS, dtype=jnp.float32))
    out = jnp.zeros(x.shape, jnp.float32)
    for e in range(N_EXPERTS):
        h = x @ wgu[e] + bgu[e]
        gate = jnp.minimum(h[:, 0::2], SWIGLU_LIMIT)
        up = jnp.clip(h[:, 1::2], -SWIGLU_LIMIT, SWIGLU_LIMIT)
        act = (up + 1.0) * gate * jax.nn.sigmoid(gate * SWIGLU_ALPHA)
        out = out + comb[:, e:e + 1] * (act @ wd[e] + bd[e]).astype(jnp.float32)
    return out.astype(x.dtype)


def setup_inputs(seed: int = 0) -> dict:
    key = jax.random.key(seed)
    ks = jax.random.split(key, 32)
    f32 = jnp.float32
    nrm = lambda k, s, sc: jax.random.normal(k, s, f32) * sc
    u = jax.random.uniform(ks[14], (DEPTH, RG_WIDTH), f32, 0.9, 0.999)
    return {
        "x_prompt": nrm(ks[0], (BATCH, SEQ, D_MODEL), 1.0),
        "x_sample": nrm(ks[1], (DEC_BATCH, DEC_SEQ, D_MODEL), 1.0),
        "state_hgrn": nrm(ks[2], (DEPTH, DEC_BATCH, HG_HEADS, HG_HEAD_DIM, HG_HEAD_DIM), 0.3),
        "state_rglru_h": nrm(ks[3], (DEPTH, DEC_BATCH, RG_WIDTH), 0.5),
        "state_rglru_conv": nrm(ks[4], (DEPTH, DEC_BATCH, RG_CONV - 1, RG_WIDTH), 1.0),
        "state_pool": nrm(ks[5], (DEPTH, DEC_BATCH, POOL_BUF, PL_WIDTH), 1.0),
        "w_in": nrm(ks[6], (DEPTH, D_MODEL, D_IN), D_MODEL ** -0.5),
        "hg_lb_logits": nrm(ks[7], (DEPTH, HG_WIDTH), 0.5),
        "hg_norm_w": 1.0 + nrm(ks[8], (DEPTH, HG_HEAD_DIM), 0.05),
        "rg_conv_w": nrm(ks[9], (DEPTH, RG_CONV, RG_WIDTH), RG_CONV ** -0.5),
        "rg_conv_b": nrm(ks[10], (DEPTH, RG_WIDTH), 0.01),
        "rg_wa": nrm(ks[11], (DEPTH, RG_BLOCKS, RG_BLOCK_DIM, RG_BLOCK_DIM), RG_BLOCK_DIM ** -0.5),
        "rg_ba": nrm(ks[12], (DEPTH, RG_WIDTH), 0.01),
        "rg_wx": nrm(ks[13], (DEPTH, RG_BLOCKS, RG_BLOCK_DIM, RG_BLOCK_DIM), RG_BLOCK_DIM ** -0.5),
        "rg_bx": nrm(ks[15], (DEPTH, RG_WIDTH), 0.01),
        "rg_lambda": jnp.log(u) - jnp.log1p(-u),
        "pool_w": nrm(ks[16], (DEPTH, PL_GROUPS, PL_GROUP_DIM, PL_GROUP_DIM), PL_GROUP_DIM ** -0.5),
        "pool_scale": 1.0 + nrm(ks[17], (DEPTH, PL_WIDTH), 0.05),
        "w_out": nrm(ks[18], (DEPTH, D_MIX, D_MODEL), D_MIX ** -0.5 * DN_BETA),
        "ln1_g": 1.0 + nrm(ks[19], (DEPTH, D_MODEL), 0.05),
        "ln1_b": nrm(ks[20], (DEPTH, D_MODEL), 0.01),
        "router_w": nrm(ks[21], (DEPTH, D_MODEL, N_EXPERTS), D_MODEL ** -0.5),
        "router_b": nrm(ks[22], (DEPTH, N_EXPERTS), 0.01),
        "moe_w_gate_up": nrm(ks[23], (DEPTH, N_EXPERTS, D_MODEL, 2 * D_FF), D_MODEL ** -0.5 * DN_BETA),
        "moe_b_gate_up": nrm(ks[24], (DEPTH, N_EXPERTS, 2 * D_FF), 0.01),
        "moe_w_down": nrm(ks[25], (DEPTH, N_EXPERTS, D_FF, D_MODEL), D_FF ** -0.5 * DN_BETA),
        "moe_b_down": nrm(ks[26], (DEPTH, N_EXPERTS, D_MODEL), 0.01),
        "ln2_g": 1.0 + nrm(ks[27], (DEPTH, D_MODEL), 0.05),
        "ln2_b": nrm(ks[28], (DEPTH, D_MODEL), 0.01),
    }


def reference(x_prompt, x_sample, state_hgrn, state_rglru_h, state_rglru_conv, state_pool,
              w_in, hg_lb_logits, hg_norm_w, rg_conv_w, rg_conv_b, rg_wa, rg_ba, rg_wx, rg_bx, rg_lambda,
              pool_w, pool_scale, w_out, ln1_g, ln1_b, router_w, router_b,
              moe_w_gate_up, moe_b_gate_up, moe_w_down, moe_b_down, ln2_g, ln2_b):
    lbs = hgrn_lower_bounds(hg_lb_logits)
    Bp, Lp, _ = x_prompt.shape
    Bs, Ls, _ = x_sample.shape
    n_p = Bp * Lp
    dt = x_prompt.dtype
    xp, xs = x_prompt, x_sample
    hg_p, hg_s, h_p, h_s, c_p, c_s, pl_p, pl_s = [], [], [], [], [], [], [], []
    for l in range(DEPTH):
        mix_w = (w_in[l], lbs[l], hg_norm_w[l], rg_conv_w[l], rg_conv_b[l], rg_wa[l], rg_ba[l],
                 rg_wx[l], rg_bx[l], rg_lambda[l], pool_w[l], pool_scale[l])
        mp, Sp, hp_, cp_, pp_ = token_mixers(
            xp, jnp.zeros((Bp, HG_HEADS, HG_HEAD_DIM, HG_HEAD_DIM), jnp.float32),
            jnp.zeros((Bp, RG_CONV - 1, RG_WIDTH), dt), jnp.zeros((Bp, RG_WIDTH), jnp.float32),
            jnp.zeros((Bp, POOL_BUF, PL_WIDTH), dt), 0, *mix_w)
        ms, Ss, hs_, cs_, ps_ = token_mixers(
            xs, state_hgrn[l], state_rglru_conv[l], state_rglru_h[l], state_pool[l], PAST_LEN, *mix_w)
        hg_p.append(Sp); hg_s.append(Ss); h_p.append(hp_); h_s.append(hs_)
        c_p.append(cp_); c_s.append(cs_); pl_p.append(pp_); pl_s.append(ps_)
        x_flat = jnp.concatenate([xp.reshape(-1, D_MODEL), xs.reshape(-1, D_MODEL)], axis=0)
        m_flat = jnp.concatenate([mp.reshape(-1, D_MIX), ms.reshape(-1, D_MIX)], axis=0)
        x_flat = layer_norm(DN_ALPHA * x_flat + m_flat @ w_out[l], ln1_g[l], ln1_b[l])
        x_flat = layer_norm(DN_ALPHA * x_flat + moe(x_flat, router_w[l], router_b[l], moe_w_gate_up[l],
                                                     moe_b_gate_up[l], moe_w_down[l], moe_b_down[l]),
                            ln2_g[l], ln2_b[l])
        xp = x_flat[:n_p].reshape(Bp, Lp, D_MODEL)
        xs = x_flat[n_p:].reshape(Bs, Ls, D_MODEL)
    return (xp, xs, jnp.stack(hg_p), jnp.stack(hg_s), jnp.stack(h_p), jnp.stack(h_s),
            jnp.stack(c_p), jnp.stack(c_s), jnp.stack(pl_p), jnp.stack(pl_s))
```

```python
import functools

import jax
import jax.numpy as jnp
from jax import lax
from jax.experimental import pallas as pl
from jax.experimental.pallas import tpu as pltpu

F32 = jnp.float32
BF16 = jnp.bfloat16
I32 = jnp.int32

HG_HEAD_DIM = 128
RG_C = 8.0
POOL_WINDOWS = (2, 4, 8, 16)
TOP_K = 4
SWIGLU_LIMIT = 7.0
SWIGLU_ALPHA = 1.702
LN_EPS = 1e-5
RMS_EPS = 1e-6
PAST_LEN = 16384
GELU_C0 = 0.7978845608028654
GELU_C1 = 0.044715

LANES = 128
SUBLANES = 8
VMEM_BYTES_V7X = 64 * 1024 * 1024

PROJ_TM = 640
PROJ_TN = 512
POST_TM = 320
HG_TILE = 128
RG_TILE = 256
DEC_ROWS = 8
MOE_TM = 1536
MOE_SUB = 256
MOE_TN = 512
COMBINE_TM = 320
NEG_BIG = -1e30


def _sigmoid(x):
    return jax.nn.sigmoid(x)


def _silu(x):
    return x * _sigmoid(x)


def _gelu_tanh(x):
    return 0.5 * x * (1.0 + jnp.tanh(GELU_C0 * (x + GELU_C1 * (x * x * x))))


def _dot(a, b):
    return jnp.dot(a.astype(BF16), b.astype(BF16), preferred_element_type=F32)


def _dot_nt(a, b):
    return lax.dot_general(a.astype(BF16), b.astype(BF16), (((1,), (1,)), ((), ())),
                           preferred_element_type=F32)


def _dot_tn(a, b):
    return lax.dot_general(a.astype(BF16), b.astype(BF16), (((0,), (0,)), ((), ())),
                           preferred_element_type=F32)


def _layer_norm(x, g, b):
    mu = jnp.mean(x, axis=-1, keepdims=True)
    xc = x - mu
    var = jnp.mean(xc * xc, axis=-1, keepdims=True)
    return xc * lax.rsqrt(var + LN_EPS) * g + b


def _in_proj_kernel(x_ref, w_ref, o_ref):
    o_ref[...] = jnp.dot(x_ref[...].astype(BF16), w_ref[...], preferred_element_type=F32)


def _in_proj(x, w_bf16):
    m, k = x.shape
    n = w_bf16.shape[1]
    return pl.pallas_call(
        _in_proj_kernel,
        grid=(m // PROJ_TM, n // PROJ_TN),
        in_specs=[pl.BlockSpec((PROJ_TM, k), lambda i, j: (i, 0)),
                  pl.BlockSpec((k, PROJ_TN), lambda i, j: (0, j))],
        out_specs=pl.BlockSpec((PROJ_TM, PROJ_TN), lambda i, j: (i, j)),
        out_shape=jax.ShapeDtypeStruct((m, n), F32),
        compiler_params=pltpu.CompilerParams(dimension_semantics=("parallel", "arbitrary")),
        name="in_proj",
    )(x, w_bf16)


def _hgrn_gates(z, log_lb, log1m_lb, one_m_lb):
    e = jnp.exp(-jnp.abs(z))
    log_sig = jnp.minimum(z, 0.0) - jnp.log1p(e)
    c = log1m_lb + log_sig
    logf = jnp.maximum(log_lb, c) + jnp.log1p(jnp.exp(-jnp.abs(log_lb - c)))
    k = one_m_lb * (jnp.where(z > 0, e, 1.0) / (1.0 + e))
    return logf, k


def _cumsum_rows(x, rid):
    n = x.shape[0]
    s = 1
    while s < n:
        x = x + jnp.where(rid >= s, pltpu.roll(x, s, axis=0), 0.0)
        s *= 2
    return x


def _hgrn_prompt_kernel(hq_ref, hf_ref, hi_ref, hg_ref, lb_ref, nw_ref, o_ref, s_out_ref, st_ref,
                        *, heads):
    tile, hd = HG_TILE, HG_HEAD_DIM
    half, quarter, mid = tile // 2, tile // 4, tile // 8
    t_idx = pl.program_id(1)
    last_t = pl.num_programs(1) - 1

    @pl.when(t_idx == 0)
    def _():
        st_ref[...] = jnp.zeros_like(st_ref)

    row = lax.broadcasted_iota(I32, (tile, tile), 0)
    col = lax.broadcasted_iota(I32, (tile, tile), 1)
    rid = lax.broadcasted_iota(I32, (tile, hd), 0)
    mask_a = (row >= half) & (col < half)
    q_shift = quarter.bit_length() - 1
    row_q, col_q = row >> q_shift, col >> q_shift
    mask_b = (row_q == col_q + 1) & ((col_q & 1) == 0)
    mask_d = (row_q == col_q) & (col <= row)

    def head_body(h, carry):
        sl = pl.ds(pl.multiple_of(h * hd, hd), hd)
        zq = hq_ref[:, sl]
        zf = hf_ref[:, sl]
        v = hi_ref[:, sl]
        zg = hg_ref[:, sl]
        q = _silu(zq)
        logf, k = _hgrn_gates(zf, lb_ref[0:1, sl], lb_ref[1:2, sl], lb_ref[2:3, sl])
        b = _cumsum_rows(logf, rid)

        def split_level(ref_rows):
            d = b - ref_rows
            e = jnp.exp(-jnp.abs(d))
            return q * jnp.where(d < 0, e, 1.0), k * jnp.where(d < 0, 1.0, e)

        qa, ka = split_level(b[half - 1:half, :])
        ref_b = jnp.where(rid < half, b[quarter - 1:quarter, :], b[half + quarter - 1:half + quarter, :])
        qb, kb = split_level(ref_b)
        ref_d = b[mid - 1:mid, :]
        for blk in range(1, 4):
            lo = blk * quarter
            ref_d = jnp.where(rid >= lo, b[lo + mid - 1:lo + mid, :], ref_d)
        dd = b - ref_d
        qd = q * jnp.exp(dd)
        kd = k * jnp.exp(-dd)

        scores = jnp.where(mask_a, _dot_nt(qa, ka),
                           jnp.where(mask_b, _dot_nt(qb, kb),
                                     jnp.where(mask_d, _dot_nt(qd, kd), 0.0)))
        st = st_ref[h]
        o = _dot(scores, v) + _dot_nt(q * jnp.exp(b), st)
        b_last = b[tile - 1:tile, :]
        st_new = st * jnp.exp(b_last) + _dot_tn(v, k * jnp.exp(b_last - b))
        st_ref[h] = st_new

        o = o * lax.rsqrt(jnp.mean(o * o, axis=-1, keepdims=True) + RMS_EPS) * nw_ref[...] * _silu(zg)
        o_ref[:, sl] = o

        @pl.when(t_idx == last_t)
        def _():
            s_out_ref[0, h] = st_new.T
        return carry

    lax.fori_loop(0, heads, head_body, 0)


def _hgrn_prompt(h_all, lb_rows, norm_w, o_buf_shape, batch, seq):
    heads = lb_rows.shape[1] // HG_HEAD_DIM
    width = heads * HG_HEAD_DIM
    n_t = seq // HG_TILE

    def hspec(j):
        return pl.BlockSpec((HG_TILE, width), lambda b, t: (b * n_t + t, j))

    return pl.pallas_call(
        functools.partial(_hgrn_prompt_kernel, heads=heads),
        grid=(batch, n_t),
        in_specs=[hspec(0), hspec(1), hspec(2), hspec(3),
                  pl.BlockSpec((3, width), lambda b, t: (0, 0)),
                  pl.BlockSpec((1, HG_HEAD_DIM), lambda b, t: (0, 0))],
        out_specs=[pl.BlockSpec((HG_TILE, width), lambda b, t: (b * n_t + t, 0)),
                   pl.BlockSpec((1, heads, HG_HEAD_DIM, HG_HEAD_DIM), lambda b, t: (b, 0, 0, 0))],
        out_shape=[jax.ShapeDtypeStruct(o_buf_shape, F32),
                   jax.ShapeDtypeStruct((batch, heads, HG_HEAD_DIM, HG_HEAD_DIM), F32)],
        scratch_shapes=[pltpu.VMEM((heads, HG_HEAD_DIM, HG_HEAD_DIM), F32)],
        compiler_params=pltpu.CompilerParams(dimension_semantics=("parallel", "arbitrary")),
        name="hgrn_prompt",
    )(h_all, h_all, h_all, h_all, lb_rows, norm_w)


def _hgrn_decode_kernel(hq_ref, hf_ref, hi_ref, hg_ref, lb_ref, nw_ref, s_ref, o_in_ref, o_ref, s_out_ref,
                        *, heads):
    del o_in_ref
    hd = HG_HEAD_DIM
    rows = DEC_ROWS
    rid = lax.broadcasted_iota(I32, (rows, hd), 0)

    for h in range(heads):
        sl = slice(h * hd, (h + 1) * hd)
        zq = hq_ref[:, sl]
        zf = hf_ref[:, sl]
        v = hi_ref[:, sl]
        zg = hg_ref[:, sl]
        q = _silu(zq)
        logf, k = _hgrn_gates(zf, lb_ref[0:1, sl], lb_ref[1:2, sl], lb_ref[2:3, sl])
        f = jnp.exp(logf)
        stack = jnp.concatenate([q, k, f, jnp.zeros((hd - 3 * rows, hd), F32)], axis=0)
        cols = stack.T
        o_rows = jnp.zeros((rows, hd), F32)
        for j in range(rows):
            q_col = cols[:, j:j + 1]
            k_col = cols[:, rows + j:rows + j + 1]
            f_col = cols[:, 2 * rows + j:2 * rows + j + 1]
            s_new = s_ref[j, h] * f_col + k_col * v[j:j + 1, :]
            s_out_ref[j, h] = s_new
            o_j = jnp.sum(s_new * q_col, axis=0, keepdims=True)
            o_rows = jnp.where(rid == j, o_j, o_rows)
        o = o_rows * lax.rsqrt(jnp.mean(o_rows * o_rows, axis=-1, keepdims=True) + RMS_EPS)
        o_ref[:, sl] = o * nw_ref[...] * _silu(zg)


def _hgrn_decode(h_all, lb_rows, norm_w, state, o_buf, row0):
    n_s, heads = state.shape[0], state.shape[1]
    width = heads * HG_HEAD_DIM
    blk0 = row0 // DEC_ROWS

    def hspec(j):
        return pl.BlockSpec((DEC_ROWS, width), lambda i: (blk0 + i, j))

    s_spec = pl.BlockSpec((DEC_ROWS, heads, HG_HEAD_DIM, HG_HEAD_DIM), lambda i: (i, 0, 0, 0))
    return pl.pallas_call(
        functools.partial(_hgrn_decode_kernel, heads=heads),
        grid=(n_s // DEC_ROWS,),
        in_specs=[hspec(0), hspec(1), hspec(2), hspec(3),
                  pl.BlockSpec((3, width), lambda i: (0, 0)),
                  pl.BlockSpec((1, HG_HEAD_DIM), lambda i: (0, 0)),
                  s_spec,
                  pl.BlockSpec(memory_space=pl.ANY)],
        out_specs=[pl.BlockSpec((DEC_ROWS, width), lambda i: (blk0 + i, 0)), s_spec],
        out_shape=[jax.ShapeDtypeStruct(o_buf.shape, F32), jax.ShapeDtypeStruct(state.shape, F32)],
        input_output_aliases={7: 0},
        compiler_params=pltpu.CompilerParams(dimension_semantics=("parallel",)),
        name="hgrn_decode",
    )(h_all, h_all, h_all, h_all, lb_rows, norm_w, state, o_buf)


def _rglru_terms(xc, gate_z, wa_ref, ba_ref, wx_ref, bx_ref, lam_ref):
    r = _sigmoid(_dot(xc, wa_ref[...]) + ba_ref[...])
    i = _sigmoid(_dot(xc, wx_ref[...]) + bx_ref[...])
    lam = lam_ref[...]
    log_sig_lam = jnp.minimum(lam, 0.0) - jnp.log1p(jnp.exp(-jnp.abs(lam)))
    log_a = RG_C * r * log_sig_lam
    a = jnp.exp(log_a)
    one_m_a2 = -jnp.tanh(log_a) * (a * a + 1.0)
    bterm = jnp.sqrt(one_m_a2) * (i * xc)
    return a, bterm, _gelu_tanh(gate_z)


def _rg_pool_prompt_kernel(rx_ref, rgate_ref, pin_ref, cw_ref, cb_ref, wa_ref, ba_ref, wx_ref, bx_ref, lam_ref,
                           pw_ref, ps_ref, o_rg_ref, o_pl_ref, h_out_ref, rx_ext, pin_ext, h_carry):
    tile = RG_TILE
    width = rx_ref.shape[1]
    conv_pad = SUBLANES
    pool_pad = 2 * SUBLANES
    t_idx = pl.program_id(1)

    @pl.when(t_idx == 0)
    def _():
        rx_ext[0:conv_pad, :] = jnp.zeros((conv_pad, width), F32)
        pin_ext[0:pool_pad, :] = jnp.zeros((pool_pad, width), F32)
        h_carry[...] = jnp.zeros_like(h_carry)

    x = rx_ref[...]
    rx_ext[conv_pad:conv_pad + tile, :] = x
    n_taps = cw_ref.shape[0]
    xc = x * cw_ref[n_taps - 1:n_taps, :] + cb_ref[...]
    for j in range(1, n_taps):
        xc = xc + rx_ext[conv_pad - j:conv_pad - j + tile, :] * cw_ref[n_taps - 1 - j:n_taps - j, :]
    rx_ext[0:conv_pad, :] = rx_ext[tile:tile + conv_pad, :]

    a, bterm, gate = _rglru_terms(xc, rgate_ref[...], wa_ref, ba_ref, wx_ref, bx_ref, lam_ref)

    rid = lax.broadcasted_iota(I32, (tile, LANES), 0)
    for blk in range(width // LANES):
        sl = slice(blk * LANES, (blk + 1) * LANES)
        a_c, b_c = a[:, sl], bterm[:, sl]
        s = 1
        while s < tile:
            a_sh = jnp.where(rid >= s, pltpu.roll(a_c, s, axis=0), 1.0)
            b_sh = jnp.where(rid >= s, pltpu.roll(b_c, s, axis=0), 0.0)
            b_c = a_c * b_sh + b_c
            a_c = a_c * a_sh
            s *= 2
        hh = a_c * h_carry[:, sl] + b_c
        h_carry[:, sl] = hh[tile - 1:tile, :]
        o_rg_ref[:, sl] = hh * gate[:, sl]

    h_out_ref[0] = h_carry[...]

    p = pin_ref[...]
    pin_ext[pool_pad:pool_pad + tile, :] = p
    pos1 = t_idx * tile + rid + 1
    group = width // len(POOL_WINDOWS)
    pooled = []
    for g, w in enumerate(POOL_WINDOWS):
        sl = slice(g * group, (g + 1) * group)
        win = p[:, sl]
        for j in range(1, w):
            win = win + pin_ext[pool_pad - j:pool_pad - j + tile, sl]
        count = jnp.minimum(pos1, w).astype(F32)
        pooled.append(win / count - p[:, sl])
    pin_ext[0:pool_pad, :] = pin_ext[tile:tile + pool_pad, :]
    o_pl_ref[...] = _dot(jnp.concatenate(pooled, axis=1), pw_ref[...]) * ps_ref[...]


def _rg_pool_prompt(h_all, rg_w, pool_w, o_rg_shape, batch, seq, col0):
    width = rg_w["lam"].shape[1]
    n_t = seq // RG_TILE
    c0 = col0 // width

    def hspec(j):
        return pl.BlockSpec((RG_TILE, width), lambda b, t: (b * n_t + t, c0 + j))

    def full(arr):
        return pl.BlockSpec(arr.shape, lambda b, t: (0,) * arr.ndim)

    weights = [rg_w["conv_w"], rg_w["conv_b"], rg_w["wa"], rg_w["ba"], rg_w["wx"], rg_w["bx"], rg_w["lam"],
               pool_w["w"], pool_w["scale"]]
    o_spec = pl.BlockSpec((RG_TILE, width), lambda b, t: (b * n_t + t, 0))
    return pl.pallas_call(
        _rg_pool_prompt_kernel,
        grid=(batch, n_t),
        in_specs=[hspec(0), hspec(1), hspec(2)] + [full(w) for w in weights],
        out_specs=[o_spec, o_spec, pl.BlockSpec((1, 1, width), lambda b, t: (b, 0, 0))],
        out_shape=[jax.ShapeDtypeStruct(o_rg_shape, F32), jax.ShapeDtypeStruct(o_rg_shape, F32),
                   jax.ShapeDtypeStruct((batch, 1, width), F32)],
        scratch_shapes=[pltpu.VMEM((RG_TILE + SUBLANES, width), F32),
                        pltpu.VMEM((RG_TILE + 2 * SUBLANES, width), F32),
                        pltpu.VMEM((1, width), F32)],
        compiler_params=pltpu.CompilerParams(dimension_semantics=("parallel", "arbitrary")),
        name="rg_pool_prompt",
    )(h_all, h_all, h_all, *weights)


def _rg_pool_decode_kernel(rx_ref, rgate_ref, pin_ref, cbuf_ref, h0_ref, pbuf_ref,
                           cw_ref, cb_ref, wa_ref, ba_ref, wx_ref, bx_ref, lam_ref, pw_ref, ps_ref,
                           o_rg_in, o_pl_in, o_rg_ref, o_pl_ref, h_out_ref):
    del o_rg_in, o_pl_in
    x = rx_ref[...]
    n_taps = cw_ref.shape[0]
    xc = x * cw_ref[n_taps - 1:n_taps, :] + cb_ref[...]
    for j in range(n_taps - 1):
        xc = xc + cbuf_ref[j] * cw_ref[j:j + 1, :]
    a, bterm, gate = _rglru_terms(xc, rgate_ref[...], wa_ref, ba_ref, wx_ref, bx_ref, lam_ref)
    hh = a * h0_ref[...] + bterm
    h_out_ref[...] = hh
    o_rg_ref[...] = hh * gate

    p = pin_ref[...]
    n_buf = pbuf_ref.shape[0]
    width = p.shape[1]
    group = width // len(POOL_WINDOWS)
    pooled = []
    for g, w in enumerate(POOL_WINDOWS):
        sl = slice(g * group, (g + 1) * group)
        win = p[:, sl]
        for j in range(1, w):
            win = win + pbuf_ref[n_buf - j, :, sl]
        pooled.append(win / float(min(PAST_LEN + 1, w)) - p[:, sl])
    o_pl_ref[...] = _dot(jnp.concatenate(pooled, axis=1), pw_ref[...]) * ps_ref[...]


def _rg_pool_decode(h_all, cbuf_t, h0, pbuf_t, rg_w, pool_w, o_rg_buf, o_pl_buf, row0, col0):
    n_s, width = h0.shape
    r0 = row0 // n_s
    c0 = col0 // width

    def hspec(j):
        return pl.BlockSpec((n_s, width), lambda i: (r0, c0 + j))

    def full(arr):
        return pl.BlockSpec(arr.shape, lambda i: (0,) * arr.ndim)

    weights = [rg_w["conv_w"], rg_w["conv_b"], rg_w["wa"], rg_w["ba"], rg_w["wx"], rg_w["bx"], rg_w["lam"],
               pool_w["w"], pool_w["scale"]]
    o_spec = pl.BlockSpec((n_s, width), lambda i: (r0, 0))
    any_spec = pl.BlockSpec(memory_space=pl.ANY)
    n_in = 6 + len(weights)
    return pl.pallas_call(
        _rg_pool_decode_kernel,
        grid=(1,),
        in_specs=[hspec(0), hspec(1), hspec(2), full(cbuf_t), full(h0), full(pbuf_t)]
        + [full(w) for w in weights] + [any_spec, any_spec],
        out_specs=[o_spec, o_spec, pl.BlockSpec((n_s, width), lambda i: (0, 0))],
        out_shape=[jax.ShapeDtypeStruct(o_rg_buf.shape, F32), jax.ShapeDtypeStruct(o_pl_buf.shape, F32),
                   jax.ShapeDtypeStruct((n_s, width), F32)],
        input_output_aliases={n_in: 0, n_in + 1: 1},
        name="rg_pool_decode",
    )(h_all, h_all, h_all, cbuf_t, h0, pbuf_t, *weights, o_rg_buf, o_pl_buf)


def _post_mix_kernel(x_ref, o_hg_ref, o_rg_ref, o_pl_ref, w_ref, g_ref, b_ref, rwh_ref, rwl_ref, rb_ref,
                     x1_ref, idx_ref, gate_ref, *, alpha, n_experts):
    n_hg = o_hg_ref.shape[1]
    n_rg = o_rg_ref.shape[1]
    y = _dot(o_hg_ref[...], w_ref[0:n_hg, :])
    y = y + _dot(o_rg_ref[...], w_ref[n_hg:n_hg + n_rg, :])
    y = y + _dot(o_pl_ref[...], w_ref[n_hg + n_rg:, :])
    x1 = _layer_norm(alpha * x_ref[...] + y, g_ref[...], b_ref[...])
    x1_ref[...] = x1

    x_hi = x1.astype(BF16)
    x_lo = (x1 - x_hi.astype(F32)).astype(BF16)
    logits = (jnp.dot(x_hi, rwh_ref[...], preferred_element_type=F32)
              + jnp.dot(x_lo, rwh_ref[...], preferred_element_type=F32)
              + jnp.dot(x_hi, rwl_ref[...], preferred_element_type=F32)) + rb_ref[...]

    lane = lax.broadcasted_iota(I32, logits.shape, 1).astype(F32)
    vals = jnp.where(lane < n_experts, logits, NEG_BIG)
    idx_out = jnp.zeros(logits.shape, F32)
    exp_out = jnp.zeros(logits.shape, F32)
    v0 = None
    denom = None
    for k in range(TOP_K):
        m = jnp.max(vals, axis=-1, keepdims=True)
        idx = jnp.min(jnp.where(vals == m, lane, float(LANES)), axis=-1, keepdims=True)
        if k == 0:
            v0 = m
        e = jnp.exp(m - v0)
        denom = e if denom is None else denom + e
        idx_out = jnp.where(lane == k, idx, idx_out)
        exp_out = jnp.where(lane == k, e, exp_out)
        vals = jnp.where(lane == idx, 2.0 * NEG_BIG, vals)
    idx_ref[...] = idx_out.astype(I32)
    gate_ref[...] = exp_out / denom


def _post_mix(x, o_hg, o_rg, o_pl, w_out_bf16, ln_g, ln_b, rw_hi, rw_lo, rb_pad, alpha, n_experts):
    m, d = x.shape

    def rows(arr):
        return pl.BlockSpec((POST_TM, arr.shape[1]), lambda i: (i, 0))

    def full(arr):
        return pl.BlockSpec(arr.shape, lambda i: (0,) * arr.ndim)

    return pl.pallas_call(
        functools.partial(_post_mix_kernel, alpha=alpha, n_experts=n_experts),
        grid=(m // POST_TM,),
        in_specs=[rows(x), rows(o_hg), rows(o_rg), rows(o_pl), full(w_out_bf16), full(ln_g), full(ln_b),
                  full(rw_hi), full(rw_lo), full(rb_pad)],
        out_specs=[pl.BlockSpec((POST_TM, d), lambda i: (i, 0)),
                   pl.BlockSpec((POST_TM, LANES), lambda i: (i, 0)),
                   pl.BlockSpec((POST_TM, LANES), lambda i: (i, 0))],
        out_shape=[jax.ShapeDtypeStruct((m, d), F32), jax.ShapeDtypeStruct((m, LANES), I32),
                   jax.ShapeDtypeStruct((m, LANES), F32)],
        compiler_params=pltpu.CompilerParams(dimension_semantics=("parallel",),
                                             vmem_limit_bytes=VMEM_BYTES_V7X * 3 // 4),
        name="post_mix",
    )(x, o_hg, o_rg, o_pl, w_out_bf16, ln_g, ln_b, rw_hi, rw_lo, rb_pad)


def _moe_kernel(item_expert, item_rows, tok_ref, dst_ref, x_hbm, wgu_ref, bgu_ref, wd_ref, bd_ref, y_hbm,
                acc_ref, xg_ref, wgu_bf, wd_bf, sem_in, sem_out):
    del item_expert
    w = pl.program_id(0)
    c = pl.program_id(1)
    n_chunks = pl.num_programs(1)
    n = item_rows[w]
    n_sub = (n + MOE_SUB - 1) // MOE_SUB
    tn = wgu_bf.shape[1]

    @pl.when(jnp.logical_and(w == 0, c == 0))
    def _():
        acc_ref[...] = jnp.zeros_like(acc_ref)
        xg_ref[...] = jnp.zeros_like(xg_ref)

    def gather_copy(r):
        return pltpu.make_async_copy(x_hbm.at[pl.ds(tok_ref[0, 0, r], 1)], acc_ref.at[pl.ds(r, 1)], sem_in)

    def scatter_copy(r):
        return pltpu.make_async_copy(acc_ref.at[pl.ds(r, 1)], y_hbm.at[pl.ds(dst_ref[0, 0, r], 1)], sem_out)

    def for_rows(fn):
        def body(r, carry):
            fn(r)
            return carry
        lax.fori_loop(0, n, body, 0)

    @pl.when(n > 0)
    def _():
        @pl.when(c == 0)
        def _():
            for_rows(lambda r: gather_copy(r).start())
            for_rows(lambda r: gather_copy(r).wait())

            def cast_body(s, carry):
                rs = pl.ds(pl.multiple_of(s * MOE_SUB, MOE_SUB), MOE_SUB)
                xg_ref[rs, :] = acc_ref[rs, :].astype(BF16)
                acc_ref[rs, :] = jnp.broadcast_to(bd_ref[0], (MOE_SUB, acc_ref.shape[1]))
                return carry
            lax.fori_loop(0, n_sub, cast_body, 0)

        wgu_bf[...] = wgu_ref[0].astype(BF16)
        wd_bf[...] = wd_ref[0].astype(BF16)
        sel_r = lax.broadcasted_iota(I32, (tn, tn // 2), 0)
        sel_c = lax.broadcasted_iota(I32, (tn, tn // 2), 1)
        pick = jnp.where(sel_r == 2 * sel_c, 1.0, 0.0).astype(BF16)

        def sub_body(s, carry):
            rs = pl.ds(pl.multiple_of(s * MOE_SUB, MOE_SUB), MOE_SUB)
            h = jnp.dot(xg_ref[rs, :], wgu_bf[...], preferred_element_type=F32) + bgu_ref[0]
            h_next = pltpu.roll(h, tn - 1, axis=1)
            gate = jnp.minimum(h, SWIGLU_LIMIT)
            up = jnp.clip(h_next, -SWIGLU_LIMIT, SWIGLU_LIMIT)
            act = (up + 1.0) * gate * _sigmoid(gate * SWIGLU_ALPHA)
            act_dense = jnp.dot(act.astype(BF16), pick, preferred_element_type=F32)
            acc_ref[rs, :] += jnp.dot(act_dense.astype(BF16), wd_bf[...], preferred_element_type=F32)
            return carry
        lax.fori_loop(0, n_sub, sub_body, 0)

        @pl.when(c == n_chunks - 1)
        def _():
            for_rows(lambda r: scatter_copy(r).start())
            for_rows(lambda r: scatter_copy(r).wait())


def _moe(x1, tables, wgu, bgu, wd, bd, n_items):
    m, d = x1.shape
    n_exp, _, n_gu = wgu.shape
    n_chunks = n_gu // MOE_TN
    half = MOE_TN // 2

    def chunk(c, rows_ref, w):
        return jnp.where(rows_ref[w] > 0, c, n_chunks - 1)

    grid_spec = pltpu.PrefetchScalarGridSpec(
        num_scalar_prefetch=2,
        grid=(n_items, n_chunks),
        in_specs=[
            pl.BlockSpec((1, 1, MOE_TM), lambda w, c, ie, ir: (w, 0, 0), memory_space=pltpu.SMEM),
            pl.BlockSpec((1, 1, MOE_TM), lambda w, c, ie, ir: (w, 0, 0), memory_space=pltpu.SMEM),
            pl.BlockSpec(memory_space=pl.ANY),
            pl.BlockSpec((1, d, MOE_TN), lambda w, c, ie, ir: (ie[w], 0, chunk(c, ir, w))),
            pl.BlockSpec((1, 1, MOE_TN), lambda w, c, ie, ir: (ie[w], 0, chunk(c, ir, w))),
            pl.BlockSpec((1, half, d), lambda w, c, ie, ir: (ie[w], chunk(c, ir, w), 0)),
            pl.BlockSpec((1, 1, d), lambda w, c, ie, ir: (ie[w], 0, 0)),
        ],
        out_specs=pl.BlockSpec(memory_space=pl.ANY),
        scratch_shapes=[pltpu.VMEM((MOE_TM, d), F32), pltpu.VMEM((MOE_TM, d), BF16),
                        pltpu.VMEM((d, MOE_TN), BF16), pltpu.VMEM((half, d), BF16),
                        pltpu.SemaphoreType.DMA, pltpu.SemaphoreType.DMA],
    )
    return pl.pallas_call(
        _moe_kernel,
        grid_spec=grid_spec,
        out_shape=jax.ShapeDtypeStruct((TOP_K * m, d), F32),
        compiler_params=pltpu.CompilerParams(dimension_semantics=("arbitrary", "arbitrary"),
                                             vmem_limit_bytes=VMEM_BYTES_V7X * 7 // 8),
        name="moe_experts",
    )(tables["expert"], tables["rows"], tables["tok"], tables["dst"], x1,
      wgu, bgu.reshape(n_exp, 1, n_gu), wd, bd.reshape(n_exp, 1, d))


def _route_tables(top_idx, n_experts, n_items):
    m = top_idx.shape[0]
    n_pairs = m * TOP_K
    pair_expert = top_idx.reshape(-1)
    order = jnp.argsort(pair_expert, stable=True).astype(I32)
    counts = jnp.sum((pair_expert[:, None] == jnp.arange(n_experts, dtype=I32)[None, :]).astype(I32), axis=0)
    row_end = jnp.cumsum(counts)
    row_off = row_end - counts
    tiles = (counts + MOE_TM - 1) // MOE_TM
    tile_end = jnp.cumsum(tiles)
    tile_off = tile_end - tiles
    total = tile_end[-1]
    w = jnp.arange(n_items, dtype=I32)
    e_w = jnp.minimum(jnp.searchsorted(tile_end, w, side="right").astype(I32), n_experts - 1)
    j = w - tile_off[e_w]
    start = row_off[e_w] + j * MOE_TM
    active = w < total
    rows = jnp.where(active, jnp.clip(counts[e_w] - j * MOE_TM, 0, MOE_TM), 0)
    e_w = jnp.where(active, e_w, e_w[jnp.maximum(total - 1, 0)])
    r = jnp.arange(MOE_TM, dtype=I32)
    pos = start[:, None] + r[None, :]
    valid = r[None, :] < rows[:, None]
    pair = order[jnp.clip(pos, 0, n_pairs - 1)]
    tok = jnp.where(valid, pair // TOP_K, 0)
    dst = jnp.where(valid, (pair % TOP_K) * m + pair // TOP_K, 0)
    return {"expert": e_w.astype(I32), "rows": rows.astype(I32),
            "tok": tok.astype(I32).reshape(n_items, 1, MOE_TM),
            "dst": dst.astype(I32).reshape(n_items, 1, MOE_TM)}


def _combine_kernel(x1_ref, gate_ref, y_ref, g_ref, b_ref, o_ref, *, alpha):
    gates = gate_ref[...]
    acc = alpha * x1_ref[...]
    for k in range(TOP_K):
        acc = acc + gates[:, k:k + 1] * y_ref[k]
    o_ref[...] = _layer_norm(acc, g_ref[...], b_ref[...])


def _combine(x1, gates, y, ln_g, ln_b, alpha):
    m, d = x1.shape
    return pl.pallas_call(
        functools.partial(_combine_kernel, alpha=alpha),
        grid=(m // COMBINE_TM,),
        in_specs=[pl.BlockSpec((COMBINE_TM, d), lambda i: (i, 0)),
                  pl.BlockSpec((COMBINE_TM, LANES), lambda i: (i, 0)),
                  pl.BlockSpec((TOP_K, COMBINE_TM, d), lambda i: (0, i, 0)),
                  pl.BlockSpec((1, d), lambda i: (0, 0)),
                  pl.BlockSpec((1, d), lambda i: (0, 0))],
        out_specs=pl.BlockSpec((COMBINE_TM, d), lambda i: (i, 0)),
        out_shape=jax.ShapeDtypeStruct((m, d), F32),
        compiler_params=pltpu.CompilerParams(dimension_semantics=("parallel",),
                                             vmem_limit_bytes=VMEM_BYTES_V7X * 3 // 4),
        name="moe_combine",
    )(x1, gates, y.reshape(TOP_K, m, d), ln_g, ln_b)


def _block_diag(w):
    nb, n, _ = w.shape
    eye = jnp.eye(nb, dtype=w.dtype)
    return (eye[:, None, :, None] * w[:, :, None, :]).reshape(nb * n, nb * n)


def _row(v):
    return v.reshape(1, -1).astype(F32)


def kernel(x_prompt, x_sample, state_hgrn, state_rglru_h, state_rglru_conv, state_pool, w_in, hg_lb_logits,
           hg_norm_w, rg_conv_w, rg_conv_b, rg_wa, rg_ba, rg_wx, rg_bx, rg_lambda, pool_w, pool_scale, w_out,
           ln1_g, ln1_b, router_w, router_b, moe_w_gate_up, moe_b_gate_up, moe_w_down, moe_b_down, ln2_g, ln2_b):
    depth = w_in.shape[0]
    bp, lp, d_model = x_prompt.shape
    bs = x_sample.shape[0]
    n_p = bp * lp
    m = n_p + bs
    hg_width = hg_lb_logits.shape[1]
    rg_width = rg_lambda.shape[1]
    n_experts = router_w.shape[2]
    alpha = float((2 * depth) ** 0.25)
    n_items = n_experts + (m * TOP_K) // MOE_TM
    rg_col0 = 4 * hg_width

    p_lb = jax.nn.softmax(hg_lb_logits.astype(F32), axis=0)
    lbs = jnp.cumsum(p_lb, axis=0)
    lbs = lbs - lbs[0]

    x = jnp.concatenate([x_prompt.reshape(n_p, d_model), x_sample.reshape(bs, d_model)], axis=0)
    outs = {k: [] for k in ("hg_p", "hg_s", "h_p", "h_s", "c_p", "c_s", "pl_p", "pl_s")}
    for l in range(depth):
        lb = lbs[l]
        lb_rows = jnp.stack([jnp.log(lb), jnp.log1p(-lb), 1.0 - lb], axis=0)
        rg_w = {"conv_w": rg_conv_w[l], "conv_b": _row(rg_conv_b[l]), "wa": _block_diag(rg_wa[l]).astype(BF16),
                "ba": _row(rg_ba[l]), "wx": _block_diag(rg_wx[l]).astype(BF16), "bx": _row(rg_bx[l]),
                "lam": _row(rg_lambda[l])}
        pl_w = {"w": _block_diag(pool_w[l]).astype(BF16), "scale": _row(pool_scale[l])}
        norm_w = _row(hg_norm_w[l])

        h_all = _in_proj(x, w_in[l].astype(BF16))

        o_hg, s_p = _hgrn_prompt(h_all, lb_rows, norm_w, (m, hg_width), bp, lp)
        o_hg, s_s = _hgrn_decode(h_all, lb_rows, norm_w, state_hgrn[l], o_hg, n_p)
        o_rg, o_pl, h_p = _rg_pool_prompt(h_all, rg_w, pl_w, (m, rg_width), bp, lp, rg_col0)
        o_rg, o_pl, h_s = _rg_pool_decode(
            h_all, jnp.swapaxes(state_rglru_conv[l], 0, 1), state_rglru_h[l], jnp.swapaxes(state_pool[l], 0, 1),
            rg_w, pl_w, o_rg, o_pl, n_p, rg_col0)

        h_prompt = h_all[:n_p].reshape(bp, lp, -1)
        rx_s = h_all[n_p:, rg_col0:rg_col0 + rg_width]
        pin_s = h_all[n_p:, rg_col0 + 2 * rg_width:]
        n_conv = state_rglru_conv.shape[2]
        n_pool = state_pool.shape[2]
        outs["hg_p"].append(s_p)
        outs["hg_s"].append(s_s)
        outs["h_p"].append(h_p.reshape(bp, rg_width))
        outs["h_s"].append(h_s)
        outs["c_p"].append(h_prompt[:, lp - n_conv:, rg_col0:rg_col0 + rg_width])
        outs["c_s"].append(jnp.concatenate([state_rglru_conv[l][:, 1:], rx_s[:, None, :]], axis=1))
        outs["pl_p"].append(h_prompt[:, lp - n_pool:, rg_col0 + 2 * rg_width:])
        outs["pl_s"].append(jnp.concatenate([state_pool[l][:, 1:], pin_s[:, None, :]], axis=1))

        rw = jnp.pad(router_w[l].astype(F32), ((0, 0), (0, LANES - n_experts)))
        rw_hi = rw.astype(BF16)
        rw_lo = (rw - rw_hi.astype(F32)).astype(BF16)
        rb_pad = jnp.pad(router_b[l].astype(F32), (0, LANES - n_experts)).reshape(1, LANES)
        x1, top_idx, gates = _post_mix(x, o_hg, o_rg, o_pl, w_out[l].astype(BF16), _row(ln1_g[l]), _row(ln1_b[l]),
                                       rw_hi, rw_lo, rb_pad, alpha, n_experts)

        tables = _route_tables(top_idx[:, :TOP_K], n_experts, n_items)
        y = _moe(x1, tables, moe_w_gate_up[l], moe_b_gate_up[l], moe_w_down[l], moe_b_down[l], n_items)
        x = _combine(x1, gates, y, _row(ln2_g[l]), _row(ln2_b[l]), alpha)

    return (x[:n_p].reshape(bp, lp, d_model), x[n_p:].reshape(bs, 1, d_model),
            jnp.stack(outs["hg_p"]), jnp.stack(outs["hg_s"]), jnp.stack(outs["h_p"]), jnp.stack(outs["h_s"]),
            jnp.stack(outs["c_p"]), jnp.stack(outs["c_s"]), jnp.stack(outs["pl_p"]), jnp.stack(outs["pl_s"]))
```

```python
import functools

import jax
import jax.numpy as jnp
from jax import lax
from jax.experimental import pallas as pl
from jax.experimental.pallas import tpu as pltpu

F32 = jnp.float32
BF16 = jnp.bfloat16
I32 = jnp.int32

HG_HEAD_DIM = 128
RG_C = 8.0
POOL_WINDOWS = (2, 4, 8, 16)
TOP_K = 4
TOP_K_SHIFT = TOP_K.bit_length() - 1
SWIGLU_LIMIT = 7.0
SWIGLU_ALPHA = 1.702
LN_EPS = 1e-5
RMS_EPS = 1e-6
PAST_LEN = 16384
GELU_C0 = 0.7978845608028654
GELU_C1 = 0.044715

LANES = 128
SUBLANES = 8
VMEM_BYTES_V7X = 64 * 1024 * 1024
SMEM_1D_TILE = 1024

PROJ_TM = 640
PROJ_TN = 1408
POST_TM = 320
MIX_TILE = 128
DEC_ROWS = 8
MOE_TM = 1536
MOE_SUB = 256
MOE_TN = 512
MOE_UNROLL = 8
MOE_IDX_LEN = -(-(MOE_TM + SMEM_1D_TILE - 1) // SMEM_1D_TILE) * SMEM_1D_TILE
COMBINE_TM = 320
NEG_BIG = -1e30

assert TOP_K == 1 << TOP_K_SHIFT


def _sigmoid(x):
    return jax.nn.sigmoid(x)


def _silu(x):
    return x * _sigmoid(x)


def _gelu_tanh(x):
    return 0.5 * x * (1.0 + jnp.tanh(GELU_C0 * (x + GELU_C1 * (x * x * x))))


def _dot(a, b):
    return jnp.dot(a.astype(BF16), b.astype(BF16), preferred_element_type=F32)


def _dot_nt(a, b):
    return lax.dot_general(a.astype(BF16), b.astype(BF16), (((1,), (1,)), ((), ())),
                           preferred_element_type=F32)


def _dot_tn(a, b):
    return lax.dot_general(a.astype(BF16), b.astype(BF16), (((0,), (0,)), ((), ())),
                           preferred_element_type=F32)


def _layer_norm(x, g, b):
    mu = jnp.mean(x, axis=-1, keepdims=True)
    xc = x - mu
    var = jnp.mean(xc * xc, axis=-1, keepdims=True)
    return xc * lax.rsqrt(var + LN_EPS) * g + b


def _in_proj_kernel(x_ref, w_ref, o_ref):
    o_ref[...] = jnp.dot(x_ref[...].astype(BF16), w_ref[...], preferred_element_type=F32)


def _in_proj(x, w_bf16):
    m, k = x.shape
    n = w_bf16.shape[1]
    return pl.pallas_call(
        _in_proj_kernel,
        grid=(m // PROJ_TM, n // PROJ_TN),
        in_specs=[pl.BlockSpec((PROJ_TM, k), lambda i, j: (i, 0)),
                  pl.BlockSpec((k, PROJ_TN), lambda i, j: (0, j))],
        out_specs=pl.BlockSpec((PROJ_TM, PROJ_TN), lambda i, j: (i, j)),
        out_shape=jax.ShapeDtypeStruct((m, n), F32),
        compiler_params=pltpu.CompilerParams(dimension_semantics=("parallel", "arbitrary"),
                                             vmem_limit_bytes=VMEM_BYTES_V7X * 3 // 4),
        name="in_proj",
    )(x, w_bf16)


def _hgrn_gates(z, log_lb, log1m_lb, one_m_lb):
    e = jnp.exp(-jnp.abs(z))
    log_sig = jnp.minimum(z, 0.0) - jnp.log1p(e)
    c = log1m_lb + log_sig
    logf = jnp.maximum(log_lb, c) + jnp.log1p(jnp.exp(-jnp.abs(log_lb - c)))
    k = one_m_lb * (jnp.where(z > 0, e, 1.0) / (1.0 + e))
    return logf, k


def _cumsum_rows(x, rid):
    n = x.shape[0]
    s = 1
    while s < n:
        x = x + jnp.where(rid >= s, pltpu.roll(x, s, axis=0), 0.0)
        s *= 2
    return x


def _hgrn_prompt_tile(hq_ref, hf_ref, hi_ref, hg_ref, lb_ref, nw_ref, o_ref, s_out_ref, st_ref,
                      t_idx, last_t, heads):
    tile, hd = MIX_TILE, HG_HEAD_DIM
    half, quarter, mid = tile // 2, tile // 4, tile // 8

    @pl.when(t_idx == 0)
    def _():
        st_ref[...] = jnp.zeros_like(st_ref)

    row = lax.broadcasted_iota(I32, (tile, tile), 0)
    col = lax.broadcasted_iota(I32, (tile, tile), 1)
    rid = lax.broadcasted_iota(I32, (tile, hd), 0)
    mask_a = (row >= half) & (col < half)
    q_shift = quarter.bit_length() - 1
    row_q, col_q = row >> q_shift, col >> q_shift
    mask_b = (row_q == col_q + 1) & ((col_q & 1) == 0)
    mask_d = (row_q == col_q) & (col <= row)

    def head_body(h, carry):
        sl = pl.ds(pl.multiple_of(h * hd, hd), hd)
        zq = hq_ref[:, sl]
        zf = hf_ref[:, sl]
        v = hi_ref[:, sl]
        zg = hg_ref[:, sl]
        q = _silu(zq)
        logf, k = _hgrn_gates(zf, lb_ref[0:1, sl], lb_ref[1:2, sl], lb_ref[2:3, sl])
        b = _cumsum_rows(logf, rid)

        def split_level(ref_rows):
            d = b - ref_rows
            e = jnp.exp(-jnp.abs(d))
            return q * jnp.where(d < 0, e, 1.0), k * jnp.where(d < 0, 1.0, e)

        qa, ka = split_level(b[half - 1:half, :])
        ref_b = jnp.where(rid < half, b[quarter - 1:quarter, :], b[half + quarter - 1:half + quarter, :])
        qb, kb = split_level(ref_b)
        ref_d = b[mid - 1:mid, :]
        for blk in range(1, 4):
            lo = blk * quarter
            ref_d = jnp.where(rid >= lo, b[lo + mid - 1:lo + mid, :], ref_d)
        dd = b - ref_d
        qd = q * jnp.exp(dd)
        kd = k * jnp.exp(-dd)

        scores = jnp.where(mask_a, _dot_nt(qa, ka),
                           jnp.where(mask_b, _dot_nt(qb, kb),
                                     jnp.where(mask_d, _dot_nt(qd, kd), 0.0)))
        st = st_ref[h]
        o = _dot(scores, v) + _dot_nt(q * jnp.exp(b), st)
        b_last = b[tile - 1:tile, :]
        st_new = st * jnp.exp(b_last) + _dot_tn(v, k * jnp.exp(b_last - b))
        st_ref[h] = st_new

        o = o * lax.rsqrt(jnp.mean(o * o, axis=-1, keepdims=True) + RMS_EPS) * nw_ref[...] * _silu(zg)
        o_ref[:, sl] = o

        @pl.when(t_idx == last_t)
        def _():
            s_out_ref[0, h] = st_new.T
        return carry

    lax.fori_loop(0, heads, head_body, 0)


def _hgrn_decode_rows(hq_ref, hf_ref, hi_ref, hg_ref, lb_ref, nw_ref, s_ref, o_ref, s_out_ref, step, heads):
    hd = HG_HEAD_DIM
    rows = DEC_ROWS
    rid = lax.broadcasted_iota(I32, (rows, hd), 0)
    rs = pl.ds(pl.multiple_of(step * rows, rows), rows)

    for h in range(heads):
        sl = slice(h * hd, (h + 1) * hd)
        zq = hq_ref[rs, sl]
        zf = hf_ref[rs, sl]
        v = hi_ref[rs, sl]
        zg = hg_ref[rs, sl]
        q = _silu(zq)
        logf, k = _hgrn_gates(zf, lb_ref[0:1, sl], lb_ref[1:2, sl], lb_ref[2:3, sl])
        f = jnp.exp(logf)
        stack = jnp.concatenate([q, k, f, jnp.zeros((hd - 3 * rows, hd), F32)], axis=0)
        cols = stack.T
        o_rows = jnp.zeros((rows, hd), F32)
        for j in range(rows):
            q_col = cols[:, j:j + 1]
            k_col = cols[:, rows + j:rows + j + 1]
            f_col = cols[:, 2 * rows + j:2 * rows + j + 1]
            s_new = s_ref[j, h] * f_col + k_col * v[j:j + 1, :]
            s_out_ref[j, h] = s_new
            o_j = jnp.sum(s_new * q_col, axis=0, keepdims=True)
            o_rows = jnp.where(rid == j, o_j, o_rows)
        o = o_rows * lax.rsqrt(jnp.mean(o_rows * o_rows, axis=-1, keepdims=True) + RMS_EPS)
        o_ref[rs, sl] = o * nw_ref[...] * _silu(zg)


def _hgrn_kernel(hq_ref, hf_ref, hi_ref, hg_ref, lb_ref, nw_ref, s_in_ref, o_ref, s_p_ref, s_s_ref, st_ref,
                 *, heads, batch, n_t, n_dec):
    b = pl.program_id(0)
    t = pl.program_id(1)

    @pl.when(jnp.logical_and(b < batch, t < n_t))
    def _():
        _hgrn_prompt_tile(hq_ref, hf_ref, hi_ref, hg_ref, lb_ref, nw_ref, o_ref, s_p_ref, st_ref,
                          t, n_t - 1, heads)

    @pl.when(jnp.logical_and(b == batch, t < n_dec))
    def _():
        _hgrn_decode_rows(hq_ref, hf_ref, hi_ref, hg_ref, lb_ref, nw_ref, s_in_ref, o_ref, s_s_ref, t, heads)


def _hgrn(h_all, lb_rows, norm_w, state, batch, seq):
    n_s, heads = state.shape[0], state.shape[1]
    assert n_s == MIX_TILE, "the sample group must fill exactly one mixer tile"
    width = heads * HG_HEAD_DIM
    n_t = seq // MIX_TILE
    n_dec = n_s // DEC_ROWS
    m = batch * seq + n_s

    def row_blk(b, t):
        return jnp.where(b < batch, b * n_t + jnp.minimum(t, n_t - 1), batch * n_t)

    def dec_blk(b, t):
        return jnp.where(b == batch, jnp.minimum(t, n_dec - 1), 0)

    def hspec(j):
        return pl.BlockSpec((MIX_TILE, width), lambda b, t: (row_blk(b, t), j))

    s_spec = pl.BlockSpec((DEC_ROWS, heads, HG_HEAD_DIM, HG_HEAD_DIM), lambda b, t: (dec_blk(b, t), 0, 0, 0))
    return pl.pallas_call(
        functools.partial(_hgrn_kernel, heads=heads, batch=batch, n_t=n_t, n_dec=n_dec),
        grid=(batch + 1, max(n_t, n_dec)),
        in_specs=[hspec(0), hspec(1), hspec(2), hspec(3),
                  pl.BlockSpec((3, width), lambda b, t: (0, 0)),
                  pl.BlockSpec((1, HG_HEAD_DIM), lambda b, t: (0, 0)),
                  s_spec],
        out_specs=[pl.BlockSpec((MIX_TILE, width), lambda b, t: (row_blk(b, t), 0)),
                   pl.BlockSpec((1, heads, HG_HEAD_DIM, HG_HEAD_DIM),
                                lambda b, t: (jnp.minimum(b, batch - 1), 0, 0, 0)),
                   s_spec],
        out_shape=[jax.ShapeDtypeStruct((m, width), F32),
                   jax.ShapeDtypeStruct((batch, heads, HG_HEAD_DIM, HG_HEAD_DIM), F32),
                   jax.ShapeDtypeStruct(state.shape, F32)],
        scratch_shapes=[pltpu.VMEM((heads, HG_HEAD_DIM, HG_HEAD_DIM), F32)],
        compiler_params=pltpu.CompilerParams(dimension_semantics=("arbitrary", "arbitrary")),
        name="hgrn",
    )(h_all, h_all, h_all, h_all, lb_rows, norm_w, state)


def _rglru_terms(xc, gate_z, wa_ref, ba_ref, wx_ref, bx_ref, lam_ref):
    r = _sigmoid(_dot(xc, wa_ref[...]) + ba_ref[...])
    i = _sigmoid(_dot(xc, wx_ref[...]) + bx_ref[...])
    lam = lam_ref[...]
    log_sig_lam = jnp.minimum(lam, 0.0) - jnp.log1p(jnp.exp(-jnp.abs(lam)))
    log_a = RG_C * r * log_sig_lam
    a = jnp.exp(log_a)
    one_m_a2 = -jnp.tanh(log_a) * (a * a + 1.0)
    bterm = jnp.sqrt(one_m_a2) * (i * xc)
    return a, bterm, _gelu_tanh(gate_z)


def _rg_pool_prompt_tile(rx_ref, rgate_ref, pin_ref, w, o_rg_ref, o_pl_ref, h_out_ref, ctail_ref, ptail_ref,
                         rx_ext, pin_ext, h_carry, t_idx):
    cw_ref, cb_ref, wa_ref, ba_ref, wx_ref, bx_ref, lam_ref, pw_ref, ps_ref = w
    tile = MIX_TILE
    width = rx_ref.shape[1]
    conv_pad = SUBLANES
    pool_pad = 2 * SUBLANES

    @pl.when(t_idx == 0)
    def _():
        rx_ext[0:conv_pad, :] = jnp.zeros((conv_pad, width), F32)
        pin_ext[0:pool_pad, :] = jnp.zeros((pool_pad, width), F32)
        h_carry[...] = jnp.zeros_like(h_carry)

    x = rx_ref[...]
    rx_ext[conv_pad:conv_pad + tile, :] = x
    n_taps = cw_ref.shape[0]
    xc = x * cw_ref[n_taps - 1:n_taps, :] + cb_ref[...]
    for j in range(1, n_taps):
        xc = xc + rx_ext[conv_pad - j:conv_pad - j + tile, :] * cw_ref[n_taps - 1 - j:n_taps - j, :]
    tail = rx_ext[tile:tile + conv_pad, :]
    rx_ext[0:conv_pad, :] = tail
    ctail_ref[0] = tail

    a, bterm, gate = _rglru_terms(xc, rgate_ref[...], wa_ref, ba_ref, wx_ref, bx_ref, lam_ref)

    rid = lax.broadcasted_iota(I32, (tile, LANES), 0)
    for blk in range(width // LANES):
        sl = slice(blk * LANES, (blk + 1) * LANES)
        a_c, b_c = a[:, sl], bterm[:, sl]
        s = 1
        while s < tile:
            a_sh = jnp.where(rid >= s, pltpu.roll(a_c, s, axis=0), 1.0)
            b_sh = jnp.where(rid >= s, pltpu.roll(b_c, s, axis=0), 0.0)
            b_c = a_c * b_sh + b_c
            a_c = a_c * a_sh
            s *= 2
        hh = a_c * h_carry[:, sl] + b_c
        h_carry[:, sl] = hh[tile - 1:tile, :]
        o_rg_ref[:, sl] = hh * gate[:, sl]

    h_out_ref[0] = h_carry[...]

    p = pin_ref[...]
    pin_ext[pool_pad:pool_pad + tile, :] = p
    pos1 = t_idx * tile + rid + 1
    group = width // len(POOL_WINDOWS)
    pooled = []
    for g, win_len in enumerate(POOL_WINDOWS):
        sl = slice(g * group, (g + 1) * group)
        win = p[:, sl]
        for j in range(1, win_len):
            win = win + pin_ext[pool_pad - j:pool_pad - j + tile, sl]
        count = jnp.minimum(pos1, win_len).astype(F32)
        pooled.append(win / count - p[:, sl])
    ptail = pin_ext[tile:tile + pool_pad, :]
    pin_ext[0:pool_pad, :] = ptail
    ptail_ref[0] = ptail
    o_pl_ref[...] = _dot(jnp.concatenate(pooled, axis=1), pw_ref[...]) * ps_ref[...]


def _rg_pool_decode_rows(rx_ref, rgate_ref, pin_ref, cbuf_ref, h0_ref, pbuf_ref, w, o_rg_ref, o_pl_ref, h_out_ref):
    cw_ref, cb_ref, wa_ref, ba_ref, wx_ref, bx_ref, lam_ref, pw_ref, ps_ref = w
    x = rx_ref[...]
    n_taps = cw_ref.shape[0]
    xc = x * cw_ref[n_taps - 1:n_taps, :] + cb_ref[...]
    for j in range(n_taps - 1):
        xc = xc + cbuf_ref[j] * cw_ref[j:j + 1, :]
    a, bterm, gate = _rglru_terms(xc, rgate_ref[...], wa_ref, ba_ref, wx_ref, bx_ref, lam_ref)
    hh = a * h0_ref[...] + bterm
    h_out_ref[...] = hh
    o_rg_ref[...] = hh * gate

    p = pin_ref[...]
    n_buf = pbuf_ref.shape[0]
    width = p.shape[1]
    group = width // len(POOL_WINDOWS)
    pooled = []
    for g, win_len in enumerate(POOL_WINDOWS):
        sl = slice(g * group, (g + 1) * group)
        win = p[:, sl]
        for j in range(1, win_len):
            win = win + pbuf_ref[n_buf - j, :, sl]
        pooled.append(win / float(min(PAST_LEN + 1, win_len)) - p[:, sl])
    o_pl_ref[...] = _dot(jnp.concatenate(pooled, axis=1), pw_ref[...]) * ps_ref[...]


def _rg_pool_kernel(rx_ref, rgate_ref, pin_ref, cbuf_ref, h0_ref, pbuf_ref,
                    cw_ref, cb_ref, wa_ref, ba_ref, wx_ref, bx_ref, lam_ref, pw_ref, ps_ref,
                    o_rg_ref, o_pl_ref, h_p_ref, ctail_ref, ptail_ref, h_s_ref,
                    rx_ext, pin_ext, h_carry, *, batch):
    b = pl.program_id(0)
    t = pl.program_id(1)
    w = (cw_ref, cb_ref, wa_ref, ba_ref, wx_ref, bx_ref, lam_ref, pw_ref, ps_ref)

    @pl.when(b < batch)
    def _():
        _rg_pool_prompt_tile(rx_ref, rgate_ref, pin_ref, w, o_rg_ref, o_pl_ref, h_p_ref, ctail_ref, ptail_ref,
                             rx_ext, pin_ext, h_carry, t)

    @pl.when(jnp.logical_and(b == batch, t == 0))
    def _():
        _rg_pool_decode_rows(rx_ref, rgate_ref, pin_ref, cbuf_ref, h0_ref, pbuf_ref, w, o_rg_ref, o_pl_ref, h_s_ref)


def _rg_pool(h_all, cbuf_t, h0, pbuf_t, rg_w, pool_w, batch, seq, col0):
    n_s, width = h0.shape
    assert n_s == MIX_TILE, "the sample group must fill exactly one mixer tile"
    n_t = seq // MIX_TILE
    c0 = col0 // width
    m = batch * seq + n_s

    def row_blk(b, t):
        return jnp.where(b < batch, b * n_t + t, batch * n_t)

    def prompt_blk(b):
        return jnp.minimum(b, batch - 1)

    def hspec(j):
        return pl.BlockSpec((MIX_TILE, width), lambda b, t: (row_blk(b, t), c0 + j))

    def full(arr):
        return pl.BlockSpec(arr.shape, lambda b, t: (0,) * arr.ndim)

    weights = [rg_w["conv_w"], rg_w["conv_b"], rg_w["wa"], rg_w["ba"], rg_w["wx"], rg_w["bx"], rg_w["lam"],
               pool_w["w"], pool_w["scale"]]
    o_spec = pl.BlockSpec((MIX_TILE, width), lambda b, t: (row_blk(b, t), 0))
    return pl.pallas_call(
        functools.partial(_rg_pool_kernel, batch=batch),
        grid=(batch + 1, n_t),
        in_specs=[hspec(0), hspec(1), hspec(2), full(cbuf_t), full(h0), full(pbuf_t)] + [full(w) for w in weights],
        out_specs=[o_spec, o_spec,
                   pl.BlockSpec((1, 1, width), lambda b, t: (prompt_blk(b), 0, 0)),
                   pl.BlockSpec((1, SUBLANES, width), lambda b, t: (prompt_blk(b), 0, 0)),
                   pl.BlockSpec((1, 2 * SUBLANES, width), lambda b, t: (prompt_blk(b), 0, 0)),
                   pl.BlockSpec((n_s, width), lambda b, t: (0, 0))],
        out_shape=[jax.ShapeDtypeStruct((m, width), F32), jax.ShapeDtypeStruct((m, width), F32),
                   jax.ShapeDtypeStruct((batch, 1, width), F32),
                   jax.ShapeDtypeStruct((batch, SUBLANES, width), F32),
                   jax.ShapeDtypeStruct((batch, 2 * SUBLANES, width), F32),
                   jax.ShapeDtypeStruct((n_s, width), F32)],
        scratch_shapes=[pltpu.VMEM((MIX_TILE + SUBLANES, width), F32),
                        pltpu.VMEM((MIX_TILE + 2 * SUBLANES, width), F32),
                        pltpu.VMEM((1, width), F32)],
        compiler_params=pltpu.CompilerParams(dimension_semantics=("arbitrary", "arbitrary")),
        name="rg_pool",
    )(h_all, h_all, h_all, cbuf_t, h0, pbuf_t, *weights)


def _post_mix_kernel(x_ref, o_hg_ref, o_rg_ref, o_pl_ref, w_ref, g_ref, b_ref, rwh_ref, rwl_ref, rb_ref,
                     x1_ref, idx_ref, gate_ref, *, alpha, n_experts):
    n_hg = o_hg_ref.shape[1]
    n_rg = o_rg_ref.shape[1]
    y = _dot(o_hg_ref[...], w_ref[0:n_hg, :])
    y = y + _dot(o_rg_ref[...], w_ref[n_hg:n_hg + n_rg, :])
    y = y + _dot(o_pl_ref[...], w_ref[n_hg + n_rg:, :])
    x1 = _layer_norm(alpha * x_ref[...] + y, g_ref[...], b_ref[...])
    x1_ref[...] = x1

    x_hi = x1.astype(BF16)
    x_lo = (x1 - x_hi.astype(F32)).astype(BF16)
    logits = (jnp.dot(x_hi, rwh_ref[...], preferred_element_type=F32)
              + jnp.dot(x_lo, rwh_ref[...], preferred_element_type=F32)
              + jnp.dot(x_hi, rwl_ref[...], preferred_element_type=F32)) + rb_ref[...]

    lane = lax.broadcasted_iota(I32, logits.shape, 1).astype(F32)
    vals = jnp.where(lane < n_experts, logits, NEG_BIG)
    idx_out = jnp.zeros(logits.shape, F32)
    exp_out = jnp.zeros(logits.shape, F32)
    v0 = None
    denom = None
    for k in range(TOP_K):
        m = jnp.max(vals, axis=-1, keepdims=True)
        idx = jnp.min(jnp.where(vals == m, lane, float(LANES)), axis=-1, keepdims=True)
        if k == 0:
            v0 = m
        e = jnp.exp(m - v0)
        denom = e if denom is None else denom + e
        idx_out = jnp.where(lane == k, idx, idx_out)
        exp_out = jnp.where(lane == k, e, exp_out)
        vals = jnp.where(lane == idx, 2.0 * NEG_BIG, vals)
    idx_ref[...] = idx_out.astype(I32)
    gate_ref[...] = exp_out / denom


def _post_mix(x, o_hg, o_rg, o_pl, w_out_bf16, ln_g, ln_b, rw_hi, rw_lo, rb_pad, alpha, n_experts):
    m, d = x.shape

    def rows(arr):
        return pl.BlockSpec((POST_TM, arr.shape[1]), lambda i: (i, 0))

    def full(arr):
        return pl.BlockSpec(arr.shape, lambda i: (0,) * arr.ndim)

    return pl.pallas_call(
        functools.partial(_post_mix_kernel, alpha=alpha, n_experts=n_experts),
        grid=(m // POST_TM,),
        in_specs=[rows(x), rows(o_hg), rows(o_rg), rows(o_pl), full(w_out_bf16), full(ln_g), full(ln_b),
                  full(rw_hi), full(rw_lo), full(rb_pad)],
        out_specs=[pl.BlockSpec((POST_TM, d), lambda i: (i, 0)),
                   pl.BlockSpec((POST_TM, LANES), lambda i: (i, 0)),
                   pl.BlockSpec((POST_TM, LANES), lambda i: (i, 0))],
        out_shape=[jax.ShapeDtypeStruct((m, d), F32), jax.ShapeDtypeStruct((m, LANES), I32),
                   jax.ShapeDtypeStruct((m, LANES), F32)],
        compiler_params=pltpu.CompilerParams(dimension_semantics=("parallel",),
                                             vmem_limit_bytes=VMEM_BYTES_V7X * 3 // 4),
        name="post_mix",
    )(x, o_hg, o_rg, o_pl, w_out_bf16, ln_g, ln_b, rw_hi, rw_lo, rb_pad)


def _for_rows(lo, hi, fn):
    groups = (hi - lo) // MOE_UNROLL

    def group_body(g, carry):
        base = pl.multiple_of(lo + g * MOE_UNROLL, MOE_UNROLL)
        for u in range(MOE_UNROLL):
            fn(base + u)
        return carry

    def single_body(r, carry):
        fn(r)
        return carry

    lax.fori_loop(0, groups, group_body, 0)
    lax.fori_loop(lo + groups * MOE_UNROLL, hi, single_body, 0)


def _moe_kernel(item_expert, item_rows, item_start, order_hbm, x_hbm, wgu_ref, bgu_ref, wd_ref, bd_ref, y_hbm,
                acc_ref, stage_ref, xg_ref, wgu_bf, wd_bf, idx_ref, sem_idx, sem_in, sem_out, *, n_tokens):
    del item_expert
    w = pl.program_id(0)
    c = pl.program_id(1)
    n_items = pl.num_programs(0)
    n_chunks = pl.num_programs(1)
    last_c = n_chunks - 1
    n = item_rows[w]
    n_sub = (n + MOE_SUB - 1) // MOE_SUB
    tn = wgu_bf.shape[1]
    slot = w & 1
    nxt = jnp.minimum(w + 1, n_items - 1)
    n_next = jnp.where(w + 1 < n_items, item_rows[nxt], 0)

    def window_offset(item):
        return item_start[item] & (SMEM_1D_TILE - 1)

    def window_copy(item, sl):
        base = pl.multiple_of(item_start[item] - window_offset(item), SMEM_1D_TILE)
        return pltpu.make_async_copy(order_hbm.at[pl.ds(base, MOE_IDX_LEN)], idx_ref.at[sl], sem_idx)

    def gather_copy(item, sl, r):
        pair = idx_ref[sl, window_offset(item) + r]
        return pltpu.make_async_copy(x_hbm.at[pl.ds(pair >> TOP_K_SHIFT, 1)], stage_ref.at[pl.ds(r, 1)], sem_in)

    def scatter_copy(r):
        pair = idx_ref[slot, window_offset(w) + r]
        dst = (pair & (TOP_K - 1)) * n_tokens + (pair >> TOP_K_SHIFT)
        return pltpu.make_async_copy(acc_ref.at[pl.ds(r, 1)], y_hbm.at[pl.ds(dst, 1)], sem_out)

    def wait_gather_row(r):
        del r
        pltpu.make_async_copy(x_hbm.at[pl.ds(0, 1)], stage_ref.at[pl.ds(0, 1)], sem_in).wait()

    def wait_scatter_row(r):
        del r
        pltpu.make_async_copy(acc_ref.at[pl.ds(0, 1)], y_hbm.at[pl.ds(0, 1)], sem_out).wait()

    @pl.when(jnp.logical_and(w == 0, c == 0))
    def _():
        stage_ref[...] = jnp.zeros_like(stage_ref)

        @pl.when(n > 0)
        def _():
            window_copy(0, 0).start()
            window_copy(0, 0).wait()
            _for_rows(0, n, lambda r: gather_copy(0, 0, r).start())

    @pl.when(n > 0)
    def _():
        @pl.when(c == 0)
        def _():
            _for_rows(0, n, wait_gather_row)

            @pl.when(w > 0)
            def _():
                _for_rows(0, item_rows[jnp.maximum(w - 1, 0)], wait_scatter_row)

            def cast_body(s, carry):
                rs = pl.ds(pl.multiple_of(s * MOE_SUB, MOE_SUB), MOE_SUB)
                xg_ref[rs, :] = stage_ref[rs, :].astype(BF16)
                acc_ref[rs, :] = jnp.broadcast_to(bd_ref[0], (MOE_SUB, acc_ref.shape[1]))
                return carry
            lax.fori_loop(0, n_sub, cast_body, 0)

            @pl.when(n_next > 0)
            def _():
                window_copy(nxt, 1 - slot).start()

        @pl.when(jnp.logical_and(c == jnp.minimum(1, last_c), n_next > 0))
        def _():
            window_copy(nxt, 1 - slot).wait()
            _for_rows(0, n_next, lambda r: gather_copy(nxt, 1 - slot, r).start())

        wgu_bf[...] = wgu_ref[0].astype(BF16)
        wd_bf[...] = wd_ref[0].astype(BF16)
        sel_r = lax.broadcasted_iota(I32, (tn, tn // 2), 0)
        sel_c = lax.broadcasted_iota(I32, (tn, tn // 2), 1)
        pick = jnp.where(sel_r == 2 * sel_c, 1.0, 0.0).astype(BF16)

        def sub_body(s, carry):
            row0 = pl.multiple_of(s * MOE_SUB, MOE_SUB)
            rs = pl.ds(row0, MOE_SUB)
            h = jnp.dot(xg_ref[rs, :], wgu_bf[...], preferred_element_type=F32) + bgu_ref[0]
            h_next = pltpu.roll(h, tn - 1, axis=1)
            gate = jnp.minimum(h, SWIGLU_LIMIT)
            up = jnp.clip(h_next, -SWIGLU_LIMIT, SWIGLU_LIMIT)
            act = (up + 1.0) * gate * _sigmoid(gate * SWIGLU_ALPHA)
            act_dense = jnp.dot(act.astype(BF16), pick, preferred_element_type=F32)
            acc_ref[rs, :] += jnp.dot(act_dense.astype(BF16), wd_bf[...], preferred_element_type=F32)

            @pl.when(c == last_c)
            def _():
                _for_rows(row0, jnp.minimum(row0 + MOE_SUB, n), lambda r: scatter_copy(r).start())
            return carry
        lax.fori_loop(0, n_sub, sub_body, 0)

        @pl.when(jnp.logical_and(c == last_c, n_next == 0))
        def _():
            _for_rows(0, n, wait_scatter_row)


def _moe(x1, tables, wgu, bgu, wd, bd, n_items):
    m, d = x1.shape
    n_exp, _, n_gu = wgu.shape
    n_chunks = n_gu // MOE_TN
    half = MOE_TN // 2

    def chunk(c, rows_ref, w):
        return jnp.where(rows_ref[w] > 0, c, n_chunks - 1)

    grid_spec = pltpu.PrefetchScalarGridSpec(
        num_scalar_prefetch=3,
        grid=(n_items, n_chunks),
        in_specs=[
            pl.BlockSpec(memory_space=pl.ANY),
            pl.BlockSpec(memory_space=pl.ANY),
            pl.BlockSpec((1, d, MOE_TN), lambda w, c, ie, ir, ist: (ie[w], 0, chunk(c, ir, w))),
            pl.BlockSpec((1, 1, MOE_TN), lambda w, c, ie, ir, ist: (ie[w], 0, chunk(c, ir, w))),
            pl.BlockSpec((1, half, d), lambda w, c, ie, ir, ist: (ie[w], chunk(c, ir, w), 0)),
            pl.BlockSpec((1, 1, d), lambda w, c, ie, ir, ist: (ie[w], 0, 0)),
        ],
        out_specs=pl.BlockSpec(memory_space=pl.ANY),
        scratch_shapes=[pltpu.VMEM((MOE_TM, d), F32), pltpu.VMEM((MOE_TM, d), F32), pltpu.VMEM((MOE_TM, d), BF16),
                        pltpu.VMEM((d, MOE_TN), BF16), pltpu.VMEM((half, d), BF16),
                        pltpu.SMEM((2, MOE_IDX_LEN), I32),
                        pltpu.SemaphoreType.DMA, pltpu.SemaphoreType.DMA, pltpu.SemaphoreType.DMA],
    )
    return pl.pallas_call(
        functools.partial(_moe_kernel, n_tokens=m),
        grid_spec=grid_spec,
        out_shape=jax.ShapeDtypeStruct((TOP_K * m, d), F32),
        compiler_params=pltpu.CompilerParams(dimension_semantics=("arbitrary", "arbitrary"),
                                             vmem_limit_bytes=VMEM_BYTES_V7X * 7 // 8),
        name="moe_experts",
    )(tables["expert"], tables["rows"], tables["start"], tables["order"], x1,
      wgu, bgu.reshape(n_exp, 1, n_gu), wd, bd.reshape(n_exp, 1, d))


def _route_tables(top_idx, n_experts, n_items):
    m = top_idx.shape[0]
    n_pairs = m * TOP_K
    pair_expert = top_idx.reshape(-1)
    order = jnp.argsort(pair_expert, stable=True).astype(I32)
    counts = jnp.sum((pair_expert[:, None] == jnp.arange(n_experts, dtype=I32)[None, :]).astype(I32), axis=0)
    row_end = jnp.cumsum(counts)
    row_off = row_end - counts
    tiles = (counts + MOE_TM - 1) // MOE_TM
    tile_end = jnp.cumsum(tiles)
    tile_off = tile_end - tiles
    total = tile_end[-1]
    w = jnp.arange(n_items, dtype=I32)
    e_w = jnp.minimum(jnp.searchsorted(tile_end, w, side="right").astype(I32), n_experts - 1)
    j = w - tile_off[e_w]
    active = w < total
    start = jnp.where(active, row_off[e_w] + j * MOE_TM, 0)
    rows = jnp.where(active, jnp.clip(counts[e_w] - j * MOE_TM, 0, MOE_TM), 0)
    e_w = jnp.where(active, e_w, e_w[jnp.maximum(total - 1, 0)])
    padded = -(-n_pairs // SMEM_1D_TILE) * SMEM_1D_TILE + MOE_IDX_LEN
    return {"expert": e_w.astype(I32), "rows": rows.astype(I32), "start": start.astype(I32),
            "order": jnp.pad(order, (0, padded - n_pairs))}


def _combine_kernel(x1_ref, gate_ref, y_ref, g_ref, b_ref, o_ref, *, alpha):
    gates = gate_ref[...]
    acc = alpha * x1_ref[...]
    for k in range(TOP_K):
        acc = acc + gates[:, k:k + 1] * y_ref[k]
    o_ref[...] = _layer_norm(acc, g_ref[...], b_ref[...])


def _combine(x1, gates, y, ln_g, ln_b, alpha):
    m, d = x1.shape
    return pl.pallas_call(
        functools.partial(_combine_kernel, alpha=alpha),
        grid=(m // COMBINE_TM,),
        in_specs=[pl.BlockSpec((COMBINE_TM, d), lambda i: (i, 0)),
                  pl.BlockSpec((COMBINE_TM, LANES), lambda i: (i, 0)),
                  pl.BlockSpec((TOP_K, COMBINE_TM, d), lambda i: (0, i, 0)),
                  pl.BlockSpec((1, d), lambda i: (0, 0)),
                  pl.BlockSpec((1, d), lambda i: (0, 0))],
        out_specs=pl.BlockSpec((COMBINE_TM, d), lambda i: (i, 0)),
        out_shape=jax.ShapeDtypeStruct((m, d), F32),
        compiler_params=pltpu.CompilerParams(dimension_semantics=("parallel",),
                                             vmem_limit_bytes=VMEM_BYTES_V7X * 3 // 4),
        name="moe_combine",
    )(x1, gates, y.reshape(TOP_K, m, d), ln_g, ln_b)


def _block_diag(w):
    nb, n, _ = w.shape
    eye = jnp.eye(nb, dtype=w.dtype)
    return (eye[:, None, :, None] * w[:, :, None, :]).reshape(nb * n, nb * n)


def _row(v):
    return v.reshape(1, -1).astype(F32)


def kernel(x_prompt, x_sample, state_hgrn, state_rglru_h, state_rglru_conv, state_pool, w_in, hg_lb_logits,
           hg_norm_w, rg_conv_w, rg_conv_b, rg_wa, rg_ba, rg_wx, rg_bx, rg_lambda, pool_w, pool_scale, w_out,
           ln1_g, ln1_b, router_w, router_b, moe_w_gate_up, moe_b_gate_up, moe_w_down, moe_b_down, ln2_g, ln2_b):
    depth = w_in.shape[0]
    bp, lp, d_model = x_prompt.shape
    bs = x_sample.shape[0]
    n_p = bp * lp
    m = n_p + bs
    hg_width = hg_lb_logits.shape[1]
    rg_width = rg_lambda.shape[1]
    n_experts = router_w.shape[2]
    n_conv = state_rglru_conv.shape[2]
    n_pool = state_pool.shape[2]
    alpha = float((2 * depth) ** 0.25)
    n_items = n_experts + (m * TOP_K) // MOE_TM
    rg_col0 = 4 * hg_width

    p_lb = jax.nn.softmax(hg_lb_logits.astype(F32), axis=0)
    lbs = jnp.cumsum(p_lb, axis=0)
    lbs = lbs - lbs[0]

    x = jnp.concatenate([x_prompt.reshape(n_p, d_model), x_sample.reshape(bs, d_model)], axis=0)
    outs = {k: [] for k in ("hg_p", "hg_s", "h_p", "h_s", "c_p", "c_s", "pl_p", "pl_s")}
    for l in range(depth):
        lb = lbs[l]
        lb_rows = jnp.stack([jnp.log(lb), jnp.log1p(-lb), 1.0 - lb], axis=0)
        rg_w = {"conv_w": rg_conv_w[l], "conv_b": _row(rg_conv_b[l]), "wa": _block_diag(rg_wa[l]).astype(BF16),
                "ba": _row(rg_ba[l]), "wx": _block_diag(rg_wx[l]).astype(BF16), "bx": _row(rg_bx[l]),
                "lam": _row(rg_lambda[l])}
        pl_w = {"w": _block_diag(pool_w[l]).astype(BF16), "scale": _row(pool_scale[l])}

        h_all = _in_proj(x, w_in[l].astype(BF16))
        o_hg, s_p, s_s = _hgrn(h_all, lb_rows, _row(hg_norm_w[l]), state_hgrn[l], bp, lp)
        o_rg, o_pl, h_p, c_tail, p_tail, h_s = _rg_pool(
            h_all, jnp.swapaxes(state_rglru_conv[l], 0, 1), state_rglru_h[l], jnp.swapaxes(state_pool[l], 0, 1),
            rg_w, pl_w, bp, lp, rg_col0)

        rx_s = h_all[n_p:, rg_col0:rg_col0 + rg_width]
        pin_s = h_all[n_p:, rg_col0 + 2 * rg_width:]
        outs["hg_p"].append(s_p)
        outs["hg_s"].append(s_s)
        outs["h_p"].append(h_p.reshape(bp, rg_width))
        outs["h_s"].append(h_s)
        outs["c_p"].append(c_tail[:, SUBLANES - n_conv:])
        outs["c_s"].append(jnp.concatenate([state_rglru_conv[l][:, 1:], rx_s[:, None, :]], axis=1))
        outs["pl_p"].append(p_tail[:, 2 * SUBLANES - n_pool:])
        outs["pl_s"].append(jnp.concatenate([state_pool[l][:, 1:], pin_s[:, None, :]], axis=1))

        rw = jnp.pad(router_w[l].astype(F32), ((0, 0), (0, LANES - n_experts)))
        rw_hi = rw.astype(BF16)
        rw_lo = (rw - rw_hi.astype(F32)).astype(BF16)
        rb_pad = jnp.pad(router_b[l].astype(F32), (0, LANES - n_experts)).reshape(1, LANES)
        x1, top_idx, gates = _post_mix(x, o_hg, o_rg, o_pl, w_out[l].astype(BF16), _row(ln1_g[l]), _row(ln1_b[l]),
                                       rw_hi, rw_lo, rb_pad, alpha, n_experts)

        tables = _route_tables(top_idx[:, :TOP_K], n_experts, n_items)
        y = _moe(x1, tables, moe_w_gate_up[l], moe_b_gate_up[l], moe_w_down[l], moe_b_down[l], n_items)
        x = _combine(x1, gates, y, _row(ln2_g[l]), _row(ln2_b[l]), alpha)

    return (x[:n_p].reshape(bp, lp, d_model), x[n_p:].reshape(bs, 1, d_model),
            jnp.stack(outs["hg_p"]), jnp.stack(outs["hg_s"]), jnp.stack(outs["h_p"]), jnp.stack(outs["h_s"]),
            jnp.stack(outs["c_p"]), jnp.stack(outs["c_s"]), jnp.stack(outs["pl_p"]), jnp.stack(outs["pl_s"]))
```

```python
import functools

import jax
import jax.numpy as jnp
from jax import lax
from jax.experimental import pallas as pl
from jax.experimental.pallas import tpu as pltpu

F32 = jnp.float32
BF16 = jnp.bfloat16
I32 = jnp.int32

HG_HEAD_DIM = 128
RG_C = 8.0
POOL_WINDOWS = (2, 4, 8, 16)
TOP_K = 4
TOP_K_SHIFT = TOP_K.bit_length() - 1
SWIGLU_LIMIT = 7.0
SWIGLU_ALPHA = 1.702
LN_EPS = 1e-5
RMS_EPS = 1e-6
PAST_LEN = 16384
GELU_C0 = 0.7978845608028654
GELU_C1 = 0.044715

LANES = 128
SUBLANES = 8
SUBLANE_SHIFT = SUBLANES.bit_length() - 1
VMEM_BYTES_V7X = 64 * 1024 * 1024
SMEM_1D_TILE = 1024

PROJ_TM = 640
PROJ_TN = 1408
POST_TM = 320
MIX_TILE = 128
DEC_ROWS = 8
MOE_TM = 1536
MOE_BLK = 128
MOE_SUB_ROWS = (512, 256, 128)
MOE_TN = 512
MOE_IDX_LEN = -(-(MOE_TM + SMEM_1D_TILE - 1) // SMEM_1D_TILE) * SMEM_1D_TILE
COMBINE_TM = 320
NEG_BIG = -1e30

assert TOP_K == 1 << TOP_K_SHIFT


def _sigmoid(x):
    return jax.nn.sigmoid(x)


def _silu(x):
    return x * _sigmoid(x)


def _gelu_tanh(x):
    return 0.5 * x * (1.0 + jnp.tanh(GELU_C0 * (x + GELU_C1 * (x * x * x))))


def _dot(a, b):
    return jnp.dot(a.astype(BF16), b.astype(BF16), preferred_element_type=F32)


def _dot_nt(a, b):
    return lax.dot_general(a.astype(BF16), b.astype(BF16), (((1,), (1,)), ((), ())),
                           preferred_element_type=F32)


def _dot_tn(a, b):
    return lax.dot_general(a.astype(BF16), b.astype(BF16), (((0,), (0,)), ((), ())),
                           preferred_element_type=F32)


def _layer_norm(x, g, b):
    mu = jnp.mean(x, axis=-1, keepdims=True)
    xc = x - mu
    var = jnp.mean(xc * xc, axis=-1, keepdims=True)
    return xc * lax.rsqrt(var + LN_EPS) * g + b


def _in_proj_kernel(x_ref, w_ref, o_ref):
    o_ref[...] = jnp.dot(x_ref[...].astype(BF16), w_ref[...], preferred_element_type=F32)


def _in_proj(x, w_bf16, layer):
    m, k = x.shape
    n = w_bf16.shape[2]
    return pl.pallas_call(
        _in_proj_kernel,
        grid=(m // PROJ_TM, n // PROJ_TN),
        in_specs=[pl.BlockSpec((PROJ_TM, k), lambda i, j: (i, 0)),
                  pl.BlockSpec((None, k, PROJ_TN), lambda i, j: (layer, 0, j))],
        out_specs=pl.BlockSpec((PROJ_TM, PROJ_TN), lambda i, j: (i, j)),
        out_shape=jax.ShapeDtypeStruct((m, n), F32),
        compiler_params=pltpu.CompilerParams(dimension_semantics=("parallel", "arbitrary"),
                                             vmem_limit_bytes=VMEM_BYTES_V7X * 3 // 4),
        name="in_proj",
    )(x, w_bf16)


def _hgrn_gates(z, log_lb, log1m_lb, one_m_lb):
    e = jnp.exp(-jnp.abs(z))
    log_sig = jnp.minimum(z, 0.0) - jnp.log1p(e)
    c = log1m_lb + log_sig
    logf = jnp.maximum(log_lb, c) + jnp.log1p(jnp.exp(-jnp.abs(log_lb - c)))
    k = one_m_lb * (jnp.where(z > 0, e, 1.0) / (1.0 + e))
    return logf, k


def _cumsum_rows(x, rid):
    n = x.shape[0]
    s = 1
    while s < n:
        x = x + jnp.where(rid >= s, pltpu.roll(x, s, axis=0), 0.0)
        s *= 2
    return x


def _hgrn_prompt_tile(hq_ref, hf_ref, hi_ref, hg_ref, lb_ref, nw_ref, o_ref, s_out_ref, st_ref,
                      t_idx, last_t, heads):
    tile, hd = MIX_TILE, HG_HEAD_DIM
    half, quarter, mid = tile // 2, tile // 4, tile // 8

    @pl.when(t_idx == 0)
    def _():
        st_ref[...] = jnp.zeros_like(st_ref)

    row = lax.broadcasted_iota(I32, (tile, tile), 0)
    col = lax.broadcasted_iota(I32, (tile, tile), 1)
    rid = lax.broadcasted_iota(I32, (tile, hd), 0)
    mask_a = (row >= half) & (col < half)
    q_shift = quarter.bit_length() - 1
    row_q, col_q = row >> q_shift, col >> q_shift
    mask_b = (row_q == col_q + 1) & ((col_q & 1) == 0)
    mask_d = (row_q == col_q) & (col <= row)

    def head_body(h, carry):
        sl = pl.ds(pl.multiple_of(h * hd, hd), hd)
        zq = hq_ref[:, sl]
        zf = hf_ref[:, sl]
        v = hi_ref[:, sl]
        zg = hg_ref[:, sl]
        q = _silu(zq)
        logf, k = _hgrn_gates(zf, lb_ref[0:1, sl], lb_ref[1:2, sl], lb_ref[2:3, sl])
        b = _cumsum_rows(logf, rid)

        def split_level(ref_rows):
            d = b - ref_rows
            e = jnp.exp(-jnp.abs(d))
            return q * jnp.where(d < 0, e, 1.0), k * jnp.where(d < 0, 1.0, e)

        qa, ka = split_level(b[half - 1:half, :])
        ref_b = jnp.where(rid < half, b[quarter - 1:quarter, :], b[half + quarter - 1:half + quarter, :])
        qb, kb = split_level(ref_b)
        ref_d = b[mid - 1:mid, :]
        for blk in range(1, 4):
            lo = blk * quarter
            ref_d = jnp.where(rid >= lo, b[lo + mid - 1:lo + mid, :], ref_d)
        dd = b - ref_d
        qd = q * jnp.exp(dd)
        kd = k * jnp.exp(-dd)

        scores = jnp.where(mask_a, _dot_nt(qa, ka),
                           jnp.where(mask_b, _dot_nt(qb, kb),
                                     jnp.where(mask_d, _dot_nt(qd, kd), 0.0)))
        st = st_ref[h]
        o = _dot(scores, v) + _dot_nt(q * jnp.exp(b), st)
        b_last = b[tile - 1:tile, :]
        st_new = st * jnp.exp(b_last) + _dot_tn(v, k * jnp.exp(b_last - b))
        st_ref[h] = st_new

        o = o * lax.rsqrt(jnp.mean(o * o, axis=-1, keepdims=True) + RMS_EPS) * nw_ref[...] * _silu(zg)
        o_ref[:, sl] = o

        @pl.when(t_idx == last_t)
        def _():
            s_out_ref[0, h] = st_new.T
        return carry

    lax.fori_loop(0, heads, head_body, 0)


def _hgrn_decode_rows(hq_ref, hf_ref, hi_ref, hg_ref, lb_ref, nw_ref, s_ref, o_ref, s_out_ref, step, heads):
    hd = HG_HEAD_DIM
    rows = DEC_ROWS
    rid = lax.broadcasted_iota(I32, (rows, hd), 0)
    rs = pl.ds(pl.multiple_of(step * rows, rows), rows)

    for h in range(heads):
        sl = slice(h * hd, (h + 1) * hd)
        zq = hq_ref[rs, sl]
        zf = hf_ref[rs, sl]
        v = hi_ref[rs, sl]
        zg = hg_ref[rs, sl]
        q = _silu(zq)
        logf, k = _hgrn_gates(zf, lb_ref[0:1, sl], lb_ref[1:2, sl], lb_ref[2:3, sl])
        f = jnp.exp(logf)
        stack = jnp.concatenate([q, k, f, jnp.zeros((hd - 3 * rows, hd), F32)], axis=0)
        cols = stack.T
        o_rows = jnp.zeros((rows, hd), F32)
        for j in range(rows):
            q_col = cols[:, j:j + 1]
            k_col = cols[:, rows + j:rows + j + 1]
            f_col = cols[:, 2 * rows + j:2 * rows + j + 1]
            s_new = s_ref[j, h] * f_col + k_col * v[j:j + 1, :]
            s_out_ref[j, h] = s_new
            o_j = jnp.sum(s_new * q_col, axis=0, keepdims=True)
            o_rows = jnp.where(rid == j, o_j, o_rows)
        o = o_rows * lax.rsqrt(jnp.mean(o_rows * o_rows, axis=-1, keepdims=True) + RMS_EPS)
        o_ref[rs, sl] = o * nw_ref[...] * _silu(zg)


def _hgrn_kernel(hq_ref, hf_ref, hi_ref, hg_ref, lb_ref, nw_ref, s_in_ref, o_ref, s_p_ref, s_s_ref, st_ref,
                 *, heads, batch, n_t, n_dec):
    b = pl.program_id(0)
    t = pl.program_id(1)

    @pl.when(jnp.logical_and(b < batch, t < n_t))
    def _():
        _hgrn_prompt_tile(hq_ref, hf_ref, hi_ref, hg_ref, lb_ref, nw_ref, o_ref, s_p_ref, st_ref,
                          t, n_t - 1, heads)

    @pl.when(jnp.logical_and(b == batch, t < n_dec))
    def _():
        _hgrn_decode_rows(hq_ref, hf_ref, hi_ref, hg_ref, lb_ref, nw_ref, s_in_ref, o_ref, s_s_ref, t, heads)


def _hgrn(h_all, lb_rows, norm_w, state, layer, batch, seq):
    n_s, heads = state.shape[1], state.shape[2]
    assert n_s == MIX_TILE, "the sample group must fill exactly one mixer tile"
    width = heads * HG_HEAD_DIM
    n_t = seq // MIX_TILE
    n_dec = n_s // DEC_ROWS
    m = batch * seq + n_s

    def row_blk(b, t):
        return jnp.where(b < batch, b * n_t + jnp.minimum(t, n_t - 1), batch * n_t)

    def dec_blk(b, t):
        return jnp.where(b == batch, jnp.minimum(t, n_dec - 1), 0)

    def hspec(j):
        return pl.BlockSpec((MIX_TILE, width), lambda b, t: (row_blk(b, t), j))

    s_shape = (DEC_ROWS, heads, HG_HEAD_DIM, HG_HEAD_DIM)
    s_in_spec = pl.BlockSpec((None,) + s_shape, lambda b, t: (layer, dec_blk(b, t), 0, 0, 0))
    s_spec = pl.BlockSpec(s_shape, lambda b, t: (dec_blk(b, t), 0, 0, 0))
    return pl.pallas_call(
        functools.partial(_hgrn_kernel, heads=heads, batch=batch, n_t=n_t, n_dec=n_dec),
        grid=(batch + 1, max(n_t, n_dec)),
        in_specs=[hspec(0), hspec(1), hspec(2), hspec(3),
                  pl.BlockSpec((3, width), lambda b, t: (0, 0)),
                  pl.BlockSpec((1, HG_HEAD_DIM), lambda b, t: (0, 0)),
                  s_in_spec],
        out_specs=[pl.BlockSpec((MIX_TILE, width), lambda b, t: (row_blk(b, t), 0)),
                   pl.BlockSpec((1, heads, HG_HEAD_DIM, HG_HEAD_DIM),
                                lambda b, t: (jnp.minimum(b, batch - 1), 0, 0, 0)),
                   s_spec],
        out_shape=[jax.ShapeDtypeStruct((m, width), F32),
                   jax.ShapeDtypeStruct((batch, heads, HG_HEAD_DIM, HG_HEAD_DIM), F32),
                   jax.ShapeDtypeStruct(state.shape[1:], F32)],
        scratch_shapes=[pltpu.VMEM((heads, HG_HEAD_DIM, HG_HEAD_DIM), F32)],
        compiler_params=pltpu.CompilerParams(dimension_semantics=("arbitrary", "arbitrary")),
        name="hgrn",
    )(h_all, h_all, h_all, h_all, lb_rows, norm_w, state)


def _rglru_terms(xc, gate_z, wa_ref, ba_ref, wx_ref, bx_ref, lam_ref):
    r = _sigmoid(_dot(xc, wa_ref[...]) + ba_ref[...])
    i = _sigmoid(_dot(xc, wx_ref[...]) + bx_ref[...])
    lam = lam_ref[...]
    log_sig_lam = jnp.minimum(lam, 0.0) - jnp.log1p(jnp.exp(-jnp.abs(lam)))
    log_a = RG_C * r * log_sig_lam
    a = jnp.exp(log_a)
    one_m_a2 = -jnp.tanh(log_a) * (a * a + 1.0)
    bterm = jnp.sqrt(one_m_a2) * (i * xc)
    return a, bterm, _gelu_tanh(gate_z)


def _rg_pool_prompt_tile(rx_ref, rgate_ref, pin_ref, w, o_rg_ref, o_pl_ref, h_out_ref, ctail_ref, ptail_ref,
                         rx_ext, pin_ext, h_carry, t_idx):
    cw_ref, cb_ref, wa_ref, ba_ref, wx_ref, bx_ref, lam_ref, pw_ref, ps_ref = w
    tile = MIX_TILE
    width = rx_ref.shape[1]
    conv_pad = SUBLANES
    pool_pad = 2 * SUBLANES

    @pl.when(t_idx == 0)
    def _():
        rx_ext[0:conv_pad, :] = jnp.zeros((conv_pad, width), F32)
        pin_ext[0:pool_pad, :] = jnp.zeros((pool_pad, width), F32)
        h_carry[...] = jnp.zeros_like(h_carry)

    x = rx_ref[...]
    rx_ext[conv_pad:conv_pad + tile, :] = x
    n_taps = cw_ref.shape[0]
    xc = x * cw_ref[n_taps - 1:n_taps, :] + cb_ref[...]
    for j in range(1, n_taps):
        xc = xc + rx_ext[conv_pad - j:conv_pad - j + tile, :] * cw_ref[n_taps - 1 - j:n_taps - j, :]
    tail = rx_ext[tile:tile + conv_pad, :]
    rx_ext[0:conv_pad, :] = tail
    ctail_ref[0] = tail

    a, bterm, gate = _rglru_terms(xc, rgate_ref[...], wa_ref, ba_ref, wx_ref, bx_ref, lam_ref)

    rid = lax.broadcasted_iota(I32, (tile, LANES), 0)
    for blk in range(width // LANES):
        sl = slice(blk * LANES, (blk + 1) * LANES)
        a_c, b_c = a[:, sl], bterm[:, sl]
        s = 1
        while s < tile:
            a_sh = jnp.where(rid >= s, pltpu.roll(a_c, s, axis=0), 1.0)
            b_sh = jnp.where(rid >= s, pltpu.roll(b_c, s, axis=0), 0.0)
            b_c = a_c * b_sh + b_c
            a_c = a_c * a_sh
            s *= 2
        hh = a_c * h_carry[:, sl] + b_c
        h_carry[:, sl] = hh[tile - 1:tile, :]
        o_rg_ref[:, sl] = hh * gate[:, sl]

    h_out_ref[0] = h_carry[...]

    p = pin_ref[...]
    pin_ext[pool_pad:pool_pad + tile, :] = p
    pos1 = t_idx * tile + rid + 1
    group = width // len(POOL_WINDOWS)
    pooled = []
    for g, win_len in enumerate(POOL_WINDOWS):
        sl = slice(g * group, (g + 1) * group)
        win = p[:, sl]
        for j in range(1, win_len):
            win = win + pin_ext[pool_pad - j:pool_pad - j + tile, sl]
        count = jnp.minimum(pos1, win_len).astype(F32)
        pooled.append(win / count - p[:, sl])
    ptail = pin_ext[tile:tile + pool_pad, :]
    pin_ext[0:pool_pad, :] = ptail
    ptail_ref[0] = ptail
    o_pl_ref[...] = _dot(jnp.concatenate(pooled, axis=1), pw_ref[...]) * ps_ref[...]


def _rg_pool_decode_rows(rx_ref, rgate_ref, pin_ref, cbuf_ref, h0_ref, pbuf_ref, w, o_rg_ref, o_pl_ref, h_out_ref):
    cw_ref, cb_ref, wa_ref, ba_ref, wx_ref, bx_ref, lam_ref, pw_ref, ps_ref = w
    x = rx_ref[...]
    n_taps = cw_ref.shape[0]
    xc = x * cw_ref[n_taps - 1:n_taps, :] + cb_ref[...]
    for j in range(n_taps - 1):
        xc = xc + cbuf_ref[j] * cw_ref[j:j + 1, :]
    a, bterm, gate = _rglru_terms(xc, rgate_ref[...], wa_ref, ba_ref, wx_ref, bx_ref, lam_ref)
    hh = a * h0_ref[...] + bterm
    h_out_ref[...] = hh
    o_rg_ref[...] = hh * gate

    p = pin_ref[...]
    n_buf = pbuf_ref.shape[0]
    width = p.shape[1]
    group = width // len(POOL_WINDOWS)
    pooled = []
    for g, win_len in enumerate(POOL_WINDOWS):
        sl = slice(g * group, (g + 1) * group)
        win = p[:, sl]
        for j in range(1, win_len):
            win = win + pbuf_ref[n_buf - j, :, sl]
        pooled.append(win / float(min(PAST_LEN + 1, win_len)) - p[:, sl])
    o_pl_ref[...] = _dot(jnp.concatenate(pooled, axis=1), pw_ref[...]) * ps_ref[...]


def _rg_pool_kernel(rx_ref, rgate_ref, pin_ref, cbuf_ref, h0_ref, pbuf_ref,
                    cw_ref, cb_ref, wa_ref, ba_ref, wx_ref, bx_ref, lam_ref, pw_ref, ps_ref,
                    o_rg_ref, o_pl_ref, h_p_ref, ctail_ref, ptail_ref, h_s_ref,
                    rx_ext, pin_ext, h_carry, *, batch):
    b = pl.program_id(0)
    t = pl.program_id(1)
    w = (cw_ref, cb_ref, wa_ref, ba_ref, wx_ref, bx_ref, lam_ref, pw_ref, ps_ref)

    @pl.when(b < batch)
    def _():
        _rg_pool_prompt_tile(rx_ref, rgate_ref, pin_ref, w, o_rg_ref, o_pl_ref, h_p_ref, ctail_ref, ptail_ref,
                             rx_ext, pin_ext, h_carry, t)

    @pl.when(jnp.logical_and(b == batch, t == 0))
    def _():
        _rg_pool_decode_rows(rx_ref, rgate_ref, pin_ref, cbuf_ref, h0_ref, pbuf_ref, w, o_rg_ref, o_pl_ref, h_s_ref)


def _rg_pool(h_all, cbuf_t, h0, pbuf_t, rg_w, pool_w, batch, seq, col0):
    n_s, width = h0.shape
    assert n_s == MIX_TILE, "the sample group must fill exactly one mixer tile"
    n_t = seq // MIX_TILE
    c0 = col0 // width
    m = batch * seq + n_s

    def row_blk(b, t):
        return jnp.where(b < batch, b * n_t + t, batch * n_t)

    def prompt_blk(b):
        return jnp.minimum(b, batch - 1)

    def hspec(j):
        return pl.BlockSpec((MIX_TILE, width), lambda b, t: (row_blk(b, t), c0 + j))

    def full(arr):
        return pl.BlockSpec(arr.shape, lambda b, t: (0,) * arr.ndim)

    weights = [rg_w["conv_w"], rg_w["conv_b"], rg_w["wa"], rg_w["ba"], rg_w["wx"], rg_w["bx"], rg_w["lam"],
               pool_w["w"], pool_w["scale"]]
    o_spec = pl.BlockSpec((MIX_TILE, width), lambda b, t: (row_blk(b, t), 0))
    return pl.pallas_call(
        functools.partial(_rg_pool_kernel, batch=batch),
        grid=(batch + 1, n_t),
        in_specs=[hspec(0), hspec(1), hspec(2), full(cbuf_t), full(h0), full(pbuf_t)] + [full(w) for w in weights],
        out_specs=[o_spec, o_spec,
                   pl.BlockSpec((1, 1, width), lambda b, t: (prompt_blk(b), 0, 0)),
                   pl.BlockSpec((1, SUBLANES, width), lambda b, t: (prompt_blk(b), 0, 0)),
                   pl.BlockSpec((1, 2 * SUBLANES, width), lambda b, t: (prompt_blk(b), 0, 0)),
                   pl.BlockSpec((n_s, width), lambda b, t: (0, 0))],
        out_shape=[jax.ShapeDtypeStruct((m, width), F32), jax.ShapeDtypeStruct((m, width), F32),
                   jax.ShapeDtypeStruct((batch, 1, width), F32),
                   jax.ShapeDtypeStruct((batch, SUBLANES, width), F32),
                   jax.ShapeDtypeStruct((batch, 2 * SUBLANES, width), F32),
                   jax.ShapeDtypeStruct((n_s, width), F32)],
        scratch_shapes=[pltpu.VMEM((MIX_TILE + SUBLANES, width), F32),
                        pltpu.VMEM((MIX_TILE + 2 * SUBLANES, width), F32),
                        pltpu.VMEM((1, width), F32)],
        compiler_params=pltpu.CompilerParams(dimension_semantics=("arbitrary", "arbitrary")),
        name="rg_pool",
    )(h_all, h_all, h_all, cbuf_t, h0, pbuf_t, *weights)


def _post_mix_kernel(x_ref, o_hg_ref, o_rg_ref, o_pl_ref, w_ref, g_ref, b_ref, rwh_ref, rwl_ref, rb_ref,
                     x1_ref, idx_ref, gate_ref, *, alpha, n_experts):
    n_hg = o_hg_ref.shape[1]
    n_rg = o_rg_ref.shape[1]
    y = _dot(o_hg_ref[...], w_ref[0:n_hg, :])
    y = y + _dot(o_rg_ref[...], w_ref[n_hg:n_hg + n_rg, :])
    y = y + _dot(o_pl_ref[...], w_ref[n_hg + n_rg:, :])
    x1 = _layer_norm(alpha * x_ref[...] + y, g_ref[...], b_ref[...])
    x1_ref[...] = x1

    x_hi = x1.astype(BF16)
    x_lo = (x1 - x_hi.astype(F32)).astype(BF16)
    logits = (jnp.dot(x_hi, rwh_ref[...], preferred_element_type=F32)
              + jnp.dot(x_lo, rwh_ref[...], preferred_element_type=F32)
              + jnp.dot(x_hi, rwl_ref[...], preferred_element_type=F32)) + rb_ref[...]

    lane = lax.broadcasted_iota(I32, logits.shape, 1).astype(F32)
    vals = jnp.where(lane < n_experts, logits, NEG_BIG)
    idx_out = jnp.zeros(logits.shape, F32)
    exp_out = jnp.zeros(logits.shape, F32)
    v0 = None
    denom = None
    for k in range(TOP_K):
        m = jnp.max(vals, axis=-1, keepdims=True)
        idx = jnp.min(jnp.where(vals == m, lane, float(LANES)), axis=-1, keepdims=True)
        if k == 0:
            v0 = m
        e = jnp.exp(m - v0)
        denom = e if denom is None else denom + e
        idx_out = jnp.where(lane == k, idx, idx_out)
        exp_out = jnp.where(lane == k, e, exp_out)
        vals = jnp.where(lane == idx, 2.0 * NEG_BIG, vals)
    idx_ref[...] = idx_out.astype(I32)
    gate_ref[...] = exp_out / denom


def _post_mix(x, o_hg, o_rg, o_pl, w_out_bf16, layer, ln_g, ln_b, rw_hi, rw_lo, rb_pad, alpha, n_experts):
    m, d = x.shape

    def rows(arr):
        return pl.BlockSpec((POST_TM, arr.shape[1]), lambda i: (i, 0))

    def full(arr):
        return pl.BlockSpec(arr.shape, lambda i: (0,) * arr.ndim)

    w_spec = pl.BlockSpec((None,) + w_out_bf16.shape[1:], lambda i: (layer, 0, 0))
    return pl.pallas_call(
        functools.partial(_post_mix_kernel, alpha=alpha, n_experts=n_experts),
        grid=(m // POST_TM,),
        in_specs=[rows(x), rows(o_hg), rows(o_rg), rows(o_pl), w_spec, full(ln_g), full(ln_b),
                  full(rw_hi), full(rw_lo), full(rb_pad)],
        out_specs=[pl.BlockSpec((POST_TM, d), lambda i: (i, 0)),
                   pl.BlockSpec((POST_TM, LANES), lambda i: (i, 0)),
                   pl.BlockSpec((POST_TM, LANES), lambda i: (i, 0))],
        out_shape=[jax.ShapeDtypeStruct((m, d), F32), jax.ShapeDtypeStruct((m, LANES), I32),
                   jax.ShapeDtypeStruct((m, LANES), F32)],
        compiler_params=pltpu.CompilerParams(dimension_semantics=("parallel",),
                                             vmem_limit_bytes=VMEM_BYTES_V7X * 3 // 4),
        name="post_mix",
    )(x, o_hg, o_rg, o_pl, w_out_bf16, ln_g, ln_b, rw_hi, rw_lo, rb_pad)


def _for_rows(lo, hi, fn):
    g_lo = lo >> SUBLANE_SHIFT
    g_hi = hi >> SUBLANE_SHIFT

    def group_body(g, carry):
        for u in range(SUBLANES):
            fn(g, u)
        return carry

    def single_body(r, carry):
        fn(g_hi, r - (g_hi << SUBLANE_SHIFT))
        return carry

    lax.fori_loop(g_lo, g_hi, group_body, 0)
    lax.fori_loop(g_hi << SUBLANE_SHIFT, hi, single_body, 0)


def _moe_kernel(item_expert, item_rows, item_start, order_hbm, x_hbm, wgu_ref, bgu_ref, wd_ref, bd_ref, y_hbm,
                acc_ref, stage_ref, xg_ref, wgu_bf, wd_bf, idx_ref, sem_idx, sem_in, sem_out, *, n_tokens):
    del item_expert
    w = pl.program_id(0)
    c = pl.program_id(1)
    n_items = pl.num_programs(0)
    n_chunks = pl.num_programs(1)
    last_c = n_chunks - 1
    n = item_rows[w]
    n_blk = (n + MOE_BLK - 1) // MOE_BLK
    d = xg_ref.shape[1]
    tn = wgu_bf.shape[1]
    blk_groups = MOE_BLK // SUBLANES
    nxt = jnp.minimum(w + 1, n_items - 1)
    n_next = jnp.where(w + 1 < n_items, item_rows[nxt], 0)
    this_win = pl.multiple_of((w & 1) * MOE_IDX_LEN, SMEM_1D_TILE)
    next_win = pl.multiple_of((1 - (w & 1)) * MOE_IDX_LEN, SMEM_1D_TILE)

    def window_offset(item):
        return item_start[item] & (SMEM_1D_TILE - 1)

    def window_copy(item, win):
        src0 = pl.multiple_of(item_start[item] - window_offset(item), SMEM_1D_TILE)
        return pltpu.make_async_copy(order_hbm.at[pl.ds(src0, MOE_IDX_LEN)], idx_ref.at[pl.ds(win, MOE_IDX_LEN)],
                                     sem_idx)

    def start_gather(item, win, count):
        first = win + window_offset(item)

        def issue(g, u):
            pair = idx_ref[first + (g << SUBLANE_SHIFT) + u]
            pltpu.make_async_copy(x_hbm.at[pl.ds(pair >> TOP_K_SHIFT, 1)], stage_ref.at[g, pl.ds(u, 1)],
                                  sem_in).start()
        _for_rows(0, count, issue)

    def start_scatter(lo, hi):
        first = this_win + window_offset(w)

        def issue(g, u):
            pair = idx_ref[first + (g << SUBLANE_SHIFT) + u]
            dst = (pair & (TOP_K - 1)) * n_tokens + (pair >> TOP_K_SHIFT)
            pltpu.make_async_copy(acc_ref.at[g, pl.ds(u, 1)], y_hbm.at[pl.ds(dst, 1)], sem_out).start()
        _for_rows(lo, hi, issue)

    def wait_gather(count):
        _for_rows(0, count, lambda g, u: pltpu.make_async_copy(
            x_hbm.at[pl.ds(0, 1)], stage_ref.at[0, pl.ds(0, 1)], sem_in).wait())

    def wait_scatter(count):
        _for_rows(0, count, lambda g, u: pltpu.make_async_copy(
            acc_ref.at[0, pl.ds(0, 1)], y_hbm.at[pl.ds(0, 1)], sem_out).wait())

    @pl.when(jnp.logical_and(w == 0, c == 0))
    def _():
        stage_ref[...] = jnp.zeros_like(stage_ref)

        @pl.when(n > 0)
        def _():
            window_copy(0, this_win).start()
            window_copy(0, this_win).wait()
            start_gather(0, this_win, n)

    @pl.when(n > 0)
    def _():
        @pl.when(c == 0)
        def _():
            wait_gather(n)

            @pl.when(w > 0)
            def _():
                wait_scatter(item_rows[jnp.maximum(w - 1, 0)])

            def cast_body(i, carry):
                gs = pl.ds(pl.multiple_of(i * blk_groups, blk_groups), blk_groups)
                rs = pl.ds(pl.multiple_of(i * MOE_BLK, MOE_BLK), MOE_BLK)
                xg_ref[rs, :] = stage_ref[gs].reshape(MOE_BLK, d).astype(BF16)
                acc_ref[gs] = jnp.broadcast_to(bd_ref[...], (blk_groups, SUBLANES, d))
                return carry
            lax.fori_loop(0, n_blk, cast_body, 0)

            @pl.when(n_next > 0)
            def _():
                window_copy(nxt, next_win).start()

        @pl.when(jnp.logical_and(c == jnp.minimum(1, last_c), n_next > 0))
        def _():
            window_copy(nxt, next_win).wait()
            start_gather(nxt, next_win, n_next)

        wgu_bf[...] = wgu_ref[...].astype(BF16)
        wd_bf[...] = wd_ref[...].astype(BF16)
        sel_r = lax.broadcasted_iota(I32, (tn, tn // 2), 0)
        sel_c = lax.broadcasted_iota(I32, (tn, tn // 2), 1)
        pick = jnp.where(sel_r == 2 * sel_c, 1.0, 0.0).astype(BF16)

        def sub_tile(row0, rows):
            groups = rows // SUBLANES
            gs = pl.ds(pl.multiple_of(row0 >> SUBLANE_SHIFT, blk_groups), groups)
            h = jnp.dot(xg_ref[pl.ds(row0, rows), :], wgu_bf[...], preferred_element_type=F32) + bgu_ref[...]
            h_next = pltpu.roll(h, tn - 1, axis=1)
            gate = jnp.minimum(h, SWIGLU_LIMIT)
            up = jnp.clip(h_next, -SWIGLU_LIMIT, SWIGLU_LIMIT)
            act = (up + 1.0) * gate * _sigmoid(gate * SWIGLU_ALPHA)
            act_dense = jnp.dot(act.astype(BF16), pick, preferred_element_type=F32)
            upd = jnp.dot(act_dense.astype(BF16), wd_bf[...], preferred_element_type=F32)
            acc_ref[gs] = acc_ref[gs] + upd.reshape(groups, SUBLANES, d)

            @pl.when(c == last_c)
            def _():
                start_scatter(row0, jnp.minimum(row0 + rows, n))

        big = MOE_SUB_ROWS[0]
        n_big = n_blk // (big // MOE_BLK)

        def big_body(i, carry):
            sub_tile(pl.multiple_of(i * big, big), big)
            return carry
        lax.fori_loop(0, n_big, big_body, 0)
        row = n_big * big
        for rows in MOE_SUB_ROWS[1:]:
            take = (n_blk & (rows // MOE_BLK)) != 0

            @pl.when(take)
            def _(row=row, rows=rows):
                sub_tile(pl.multiple_of(row, MOE_BLK), rows)
            row = row + jnp.where(take, rows, 0)

        @pl.when(jnp.logical_and(c == last_c, n_next == 0))
        def _():
            wait_scatter(n)


def _moe(x1, tables, wgu, bgu, wd, bd, layer, n_items):
    m, d = x1.shape
    depth, n_exp, _, n_gu = wgu.shape
    n_chunks = n_gu // MOE_TN
    half = MOE_TN // 2
    row_groups = MOE_TM // SUBLANES

    def chunk(c, rows_ref, w):
        return jnp.where(rows_ref[w] > 0, c, n_chunks - 1)

    grid_spec = pltpu.PrefetchScalarGridSpec(
        num_scalar_prefetch=3,
        grid=(n_items, n_chunks),
        in_specs=[
            pl.BlockSpec(memory_space=pl.ANY),
            pl.BlockSpec(memory_space=pl.ANY),
            pl.BlockSpec((None, None, d, MOE_TN), lambda w, c, ie, ir, ist: (layer, ie[w], 0, chunk(c, ir, w))),
            pl.BlockSpec((None, None, 1, MOE_TN), lambda w, c, ie, ir, ist: (layer, ie[w], 0, chunk(c, ir, w))),
            pl.BlockSpec((None, None, half, d), lambda w, c, ie, ir, ist: (layer, ie[w], chunk(c, ir, w), 0)),
            pl.BlockSpec((None, None, 1, d), lambda w, c, ie, ir, ist: (layer, ie[w], 0, 0)),
        ],
        out_specs=pl.BlockSpec(memory_space=pl.ANY),
        scratch_shapes=[pltpu.VMEM((row_groups, SUBLANES, d), F32), pltpu.VMEM((row_groups, SUBLANES, d), F32),
                        pltpu.VMEM((MOE_TM, d), BF16),
                        pltpu.VMEM((d, MOE_TN), BF16), pltpu.VMEM((half, d), BF16),
                        pltpu.SMEM((2 * MOE_IDX_LEN,), I32),
                        pltpu.SemaphoreType.DMA, pltpu.SemaphoreType.DMA, pltpu.SemaphoreType.DMA],
    )
    return pl.pallas_call(
        functools.partial(_moe_kernel, n_tokens=m),
        grid_spec=grid_spec,
        out_shape=jax.ShapeDtypeStruct((TOP_K * m, d), F32),
        compiler_params=pltpu.CompilerParams(dimension_semantics=("arbitrary", "arbitrary"),
                                             vmem_limit_bytes=VMEM_BYTES_V7X * 7 // 8),
        name="moe_experts",
    )(tables["expert"], tables["rows"], tables["start"], tables["order"], x1,
      wgu, bgu.reshape(depth, n_exp, 1, n_gu), wd, bd.reshape(depth, n_exp, 1, d))


def _route_tables(top_idx, n_experts, n_items):
    m = top_idx.shape[0]
    n_pairs = m * TOP_K
    pair_expert = top_idx.reshape(-1)
    order = jnp.argsort(pair_expert, stable=True).astype(I32)
    counts = jnp.sum((pair_expert[:, None] == jnp.arange(n_experts, dtype=I32)[None, :]).astype(I32), axis=0)
    row_end = jnp.cumsum(counts)
    row_off = row_end - counts
    tiles = (counts + MOE_TM - 1) // MOE_TM
    tile_end = jnp.cumsum(tiles)
    tile_off = tile_end - tiles
    total = tile_end[-1]
    w = jnp.arange(n_items, dtype=I32)
    e_w = jnp.minimum(jnp.searchsorted(tile_end, w, side="right").astype(I32), n_experts - 1)
    j = w - tile_off[e_w]
    active = w < total
    start = jnp.where(active, row_off[e_w] + j * MOE_TM, 0)
    rows = jnp.where(active, jnp.clip(counts[e_w] - j * MOE_TM, 0, MOE_TM), 0)
    e_w = jnp.where(active, e_w, e_w[jnp.maximum(total - 1, 0)])
    padded = -(-n_pairs // SMEM_1D_TILE) * SMEM_1D_TILE + MOE_IDX_LEN
    return {"expert": e_w.astype(I32), "rows": rows.astype(I32), "start": start.astype(I32),
            "order": jnp.pad(order, (0, padded - n_pairs))}


def _combine_kernel(x1_ref, gate_ref, y_ref, g_ref, b_ref, o_ref, *, alpha):
    gates = gate_ref[...]
    acc = alpha * x1_ref[...]
    for k in range(TOP_K):
        acc = acc + gates[:, k:k + 1] * y_ref[k]
    o_ref[...] = _layer_norm(acc, g_ref[...], b_ref[...])


def _combine(x1, gates, y, ln_g, ln_b, alpha):
    m, d = x1.shape
    return pl.pallas_call(
        functools.partial(_combine_kernel, alpha=alpha),
        grid=(m // COMBINE_TM,),
        in_specs=[pl.BlockSpec((COMBINE_TM, d), lambda i: (i, 0)),
                  pl.BlockSpec((COMBINE_TM, LANES), lambda i: (i, 0)),
                  pl.BlockSpec((TOP_K, COMBINE_TM, d), lambda i: (0, i, 0)),
                  pl.BlockSpec((1, d), lambda i: (0, 0)),
                  pl.BlockSpec((1, d), lambda i: (0, 0))],
        out_specs=pl.BlockSpec((COMBINE_TM, d), lambda i: (i, 0)),
        out_shape=jax.ShapeDtypeStruct((m, d), F32),
        compiler_params=pltpu.CompilerParams(dimension_semantics=("parallel",),
                                             vmem_limit_bytes=VMEM_BYTES_V7X * 3 // 4),
        name="moe_combine",
    )(x1, gates, y.reshape(TOP_K, m, d), ln_g, ln_b)


def _block_diag(w):
    nb, n, _ = w.shape
    eye = jnp.eye(nb, dtype=w.dtype)
    return (eye[:, None, :, None] * w[:, :, None, :]).reshape(nb * n, nb * n)


def _row(v):
    return v.reshape(1, -1).astype(F32)


def kernel(x_prompt, x_sample, state_hgrn, state_rglru_h, state_rglru_conv, state_pool, w_in, hg_lb_logits,
           hg_norm_w, rg_conv_w, rg_conv_b, rg_wa, rg_ba, rg_wx, rg_bx, rg_lambda, pool_w, pool_scale, w_out,
           ln1_g, ln1_b, router_w, router_b, moe_w_gate_up, moe_b_gate_up, moe_w_down, moe_b_down, ln2_g, ln2_b):
    depth = w_in.shape[0]
    bp, lp, d_model = x_prompt.shape
    bs = x_sample.shape[0]
    n_p = bp * lp
    m = n_p + bs
    hg_width = hg_lb_logits.shape[1]
    rg_width = rg_lambda.shape[1]
    n_experts = router_w.shape[2]
    n_conv = state_rglru_conv.shape[2]
    n_pool = state_pool.shape[2]
    alpha = float((2 * depth) ** 0.25)
    n_items = n_experts + (m * TOP_K) // MOE_TM
    rg_col0 = 4 * hg_width

    p_lb = jax.nn.softmax(hg_lb_logits.astype(F32), axis=0)
    lbs = jnp.cumsum(p_lb, axis=0)
    lbs = lbs - lbs[0]

    w_in_bf16 = w_in.astype(BF16)
    w_out_bf16 = w_out.astype(BF16)
    x = jnp.concatenate([x_prompt.reshape(n_p, d_model), x_sample.reshape(bs, d_model)], axis=0)
    outs = {k: [] for k in ("hg_p", "hg_s", "h_p", "h_s", "c_p", "c_s", "pl_p", "pl_s")}
    for l in range(depth):
        lb = lbs[l]
        lb_rows = jnp.stack([jnp.log(lb), jnp.log1p(-lb), 1.0 - lb], axis=0)
        rg_w = {"conv_w": rg_conv_w[l], "conv_b": _row(rg_conv_b[l]), "wa": _block_diag(rg_wa[l]).astype(BF16),
                "ba": _row(rg_ba[l]), "wx": _block_diag(rg_wx[l]).astype(BF16), "bx": _row(rg_bx[l]),
                "lam": _row(rg_lambda[l])}
        pl_w = {"w": _block_diag(pool_w[l]).astype(BF16), "scale": _row(pool_scale[l])}

        h_all = _in_proj(x, w_in_bf16, l)
        o_hg, s_p, s_s = _hgrn(h_all, lb_rows, _row(hg_norm_w[l]), state_hgrn, l, bp, lp)
        o_rg, o_pl, h_p, c_tail, p_tail, h_s = _rg_pool(
            h_all, jnp.swapaxes(state_rglru_conv[l], 0, 1), state_rglru_h[l], jnp.swapaxes(state_pool[l], 0, 1),
            rg_w, pl_w, bp, lp, rg_col0)

        rx_s = h_all[n_p:, rg_col0:rg_col0 + rg_width]
        pin_s = h_all[n_p:, rg_col0 + 2 * rg_width:]
        outs["hg_p"].append(s_p)
        outs["hg_s"].append(s_s)
        outs["h_p"].append(h_p.reshape(bp, rg_width))
        outs["h_s"].append(h_s)
        outs["c_p"].append(c_tail[:, SUBLANES - n_conv:])
        outs["c_s"].append(jnp.concatenate([state_rglru_conv[l][:, 1:], rx_s[:, None, :]], axis=1))
        outs["pl_p"].append(p_tail[:, 2 * SUBLANES - n_pool:])
        outs["pl_s"].append(jnp.concatenate([state_pool[l][:, 1:], pin_s[:, None, :]], axis=1))

        rw = jnp.pad(router_w[l].astype(F32), ((0, 0), (0, LANES - n_experts)))
        rw_hi = rw.astype(BF16)
        rw_lo = (rw - rw_hi.astype(F32)).astype(BF16)
        rb_pad = jnp.pad(router_b[l].astype(F32), (0, LANES - n_experts)).reshape(1, LANES)
        x1, top_idx, gates = _post_mix(x, o_hg, o_rg, o_pl, w_out_bf16, l, _row(ln1_g[l]), _row(ln1_b[l]),
                                       rw_hi, rw_lo, rb_pad, alpha, n_experts)

        tables = _route_tables(top_idx[:, :TOP_K], n_experts, n_items)
        y = _moe(x1, tables, moe_w_gate_up, moe_b_gate_up, moe_w_down, moe_b_down, l, n_items)
        x = _combine(x1, gates, y, _row(ln2_g[l]), _row(ln2_b[l]), alpha)

    return (x[:n_p].reshape(bp, lp, d_model), x[n_p:].reshape(bs, 1, d_model),
            jnp.stack(outs["hg_p"]), jnp.stack(outs["hg_s"]), jnp.stack(outs["h_p"]), jnp.stack(outs["h_s"]),
            jnp.stack(outs["c_p"]), jnp.stack(outs["c_s"]), jnp.stack(outs["pl_p"]), jnp.stack(outs["pl_s"]))
```

```python
import functools

import jax
import jax.numpy as jnp
from jax import lax
from jax.experimental import pallas as pl
from jax.experimental.pallas import tpu as pltpu

F32 = jnp.float32
BF16 = jnp.bfloat16
I32 = jnp.int32

HG_HEAD_DIM = 128
RG_C = 8.0
POOL_WINDOWS = (2, 4, 8, 16)
TOP_K = 4
TOP_K_SHIFT = TOP_K.bit_length() - 1
SWIGLU_LIMIT = 7.0
SWIGLU_ALPHA = 1.702
LN_EPS = 1e-5
RMS_EPS = 1e-6
PAST_LEN = 16384
GELU_C0 = 0.7978845608028654
GELU_C1 = 0.044715

LANES = 128
SUBLANES = 8
SUBLANE_SHIFT = SUBLANES.bit_length() - 1
VMEM_BYTES_V7X = 64 * 1024 * 1024
SMEM_1D_TILE = 1024

PROJ_TM = 640
PROJ_TN = 1408
POST_TM = 320
MIX_TILE = 128
DEC_ROWS = 8
MOE_TM = 1536
MOE_BLK = 128
MOE_SUB_ROWS = (512, 256, 128)
MOE_TN = 512
MOE_IDX_LEN = -(-(MOE_TM + SMEM_1D_TILE - 1) // SMEM_1D_TILE) * SMEM_1D_TILE
COMBINE_TM = 320
NEG_BIG = -1e30

assert TOP_K == 1 << TOP_K_SHIFT


def _sigmoid(x):
    return jax.nn.sigmoid(x)


def _silu(x):
    return x * _sigmoid(x)


def _gelu_tanh(x):
    return 0.5 * x * (1.0 + jnp.tanh(GELU_C0 * (x + GELU_C1 * (x * x * x))))


def _dot(a, b):
    return jnp.dot(a.astype(BF16), b.astype(BF16), preferred_element_type=F32)


def _dot_nt(a, b):
    return lax.dot_general(a.astype(BF16), b.astype(BF16), (((1,), (1,)), ((), ())),
                           preferred_element_type=F32)


def _dot_tn(a, b):
    return lax.dot_general(a.astype(BF16), b.astype(BF16), (((0,), (0,)), ((), ())),
                           preferred_element_type=F32)


def _layer_norm(x, g, b):
    mu = jnp.mean(x, axis=-1, keepdims=True)
    xc = x - mu
    var = jnp.mean(xc * xc, axis=-1, keepdims=True)
    return xc * lax.rsqrt(var + LN_EPS) * g + b


def _in_proj_kernel(x_ref, w_ref, o_ref):
    o_ref[...] = jnp.dot(x_ref[...].astype(BF16), w_ref[...], preferred_element_type=F32)


def _in_proj(x, w_bf16, layer):
    m, k = x.shape
    n = w_bf16.shape[2]
    return pl.pallas_call(
        _in_proj_kernel,
        grid=(m // PROJ_TM, n // PROJ_TN),
        in_specs=[pl.BlockSpec((PROJ_TM, k), lambda i, j: (i, 0)),
                  pl.BlockSpec((None, k, PROJ_TN), lambda i, j: (layer, 0, j))],
        out_specs=pl.BlockSpec((PROJ_TM, PROJ_TN), lambda i, j: (i, j)),
        out_shape=jax.ShapeDtypeStruct((m, n), F32),
        compiler_params=pltpu.CompilerParams(dimension_semantics=("parallel", "arbitrary"),
                                             vmem_limit_bytes=VMEM_BYTES_V7X * 3 // 4),
        name="in_proj",
    )(x, w_bf16)


def _hgrn_gates(z, log_lb, log1m_lb, one_m_lb):
    e = jnp.exp(-jnp.abs(z))
    log_sig = jnp.minimum(z, 0.0) - jnp.log1p(e)
    c = log1m_lb + log_sig
    logf = jnp.maximum(log_lb, c) + jnp.log1p(jnp.exp(-jnp.abs(log_lb - c)))
    k = one_m_lb * (jnp.where(z > 0, e, 1.0) / (1.0 + e))
    return logf, k


def _cumsum_rows(x, rid):
    n = x.shape[0]
    s = 1
    while s < n:
        x = x + jnp.where(rid >= s, pltpu.roll(x, s, axis=0), 0.0)
        s *= 2
    return x


def _hgrn_prompt_tile(hq_ref, hf_ref, hi_ref, hg_ref, lb_ref, nw_ref, o_ref, s_out_ref, st_ref,
                      t_idx, last_t, heads):
    tile, hd = MIX_TILE, HG_HEAD_DIM
    half, quarter, mid = tile // 2, tile // 4, tile // 8

    @pl.when(t_idx == 0)
    def _():
        st_ref[...] = jnp.zeros_like(st_ref)

    row = lax.broadcasted_iota(I32, (tile, tile), 0)
    col = lax.broadcasted_iota(I32, (tile, tile), 1)
    rid = lax.broadcasted_iota(I32, (tile, hd), 0)
    mask_a = (row >= half) & (col < half)
    q_shift = quarter.bit_length() - 1
    row_q, col_q = row >> q_shift, col >> q_shift
    mask_b = (row_q == col_q + 1) & ((col_q & 1) == 0)
    mask_d = (row_q == col_q) & (col <= row)

    def head_body(h):
        sl = slice(h * hd, (h + 1) * hd)
        zq = hq_ref[:, sl]
        zf = hf_ref[:, sl]
        v = hi_ref[:, sl]
        zg = hg_ref[:, sl]
        q = _silu(zq)
        logf, k = _hgrn_gates(zf, lb_ref[0:1, sl], lb_ref[1:2, sl], lb_ref[2:3, sl])
        b = _cumsum_rows(logf, rid)

        def split_level(ref_rows):
            d = b - ref_rows
            e = jnp.exp(-jnp.abs(d))
            return q * jnp.where(d < 0, e, 1.0), k * jnp.where(d < 0, 1.0, e)

        qa, ka = split_level(b[half - 1:half, :])
        ref_b = jnp.where(rid < half, b[quarter - 1:quarter, :], b[half + quarter - 1:half + quarter, :])
        qb, kb = split_level(ref_b)
        ref_d = b[mid - 1:mid, :]
        for blk in range(1, 4):
            lo = blk * quarter
            ref_d = jnp.where(rid >= lo, b[lo + mid - 1:lo + mid, :], ref_d)
        dd = b - ref_d
        qd = q * jnp.exp(dd)
        kd = k * jnp.exp(-dd)

        scores = jnp.where(mask_a, _dot_nt(qa, ka),
                           jnp.where(mask_b, _dot_nt(qb, kb),
                                     jnp.where(mask_d, _dot_nt(qd, kd), 0.0)))
        st = st_ref[h]
        o = _dot(scores, v) + _dot_nt(q * jnp.exp(b), st)
        b_last = b[tile - 1:tile, :]
        st_new = st * jnp.exp(b_last) + _dot_tn(v, k * jnp.exp(b_last - b))
        st_ref[h] = st_new

        o = o * lax.rsqrt(jnp.mean(o * o, axis=-1, keepdims=True) + RMS_EPS) * nw_ref[...] * _silu(zg)
        o_ref[:, sl] = o

    for h in range(heads):
        head_body(h)

    @pl.when(t_idx == last_t)
    def _():
        for h in range(heads):
            s_out_ref[0, h] = st_ref[h].T


def _hgrn_decode_rows(hq_ref, hf_ref, hi_ref, hg_ref, lb_ref, nw_ref, s_ref, o_ref, s_out_ref, step, heads):
    hd = HG_HEAD_DIM
    rows = DEC_ROWS
    rid = lax.broadcasted_iota(I32, (rows, hd), 0)
    rs = pl.ds(pl.multiple_of(step * rows, rows), rows)

    for h in range(heads):
        sl = slice(h * hd, (h + 1) * hd)
        zq = hq_ref[rs, sl]
        zf = hf_ref[rs, sl]
        v = hi_ref[rs, sl]
        zg = hg_ref[rs, sl]
        q = _silu(zq)
        logf, k = _hgrn_gates(zf, lb_ref[0:1, sl], lb_ref[1:2, sl], lb_ref[2:3, sl])
        f = jnp.exp(logf)
        stack = jnp.concatenate([q, k, f, jnp.zeros((hd - 3 * rows, hd), F32)], axis=0)
        cols = stack.T
        o_rows = jnp.zeros((rows, hd), F32)
        for j in range(rows):
            q_col = cols[:, j:j + 1]
            k_col = cols[:, rows + j:rows + j + 1]
            f_col = cols[:, 2 * rows + j:2 * rows + j + 1]
            s_new = s_ref[j, h] * f_col + k_col * v[j:j + 1, :]
            s_out_ref[j, h] = s_new
            o_j = jnp.sum(s_new * q_col, axis=0, keepdims=True)
            o_rows = jnp.where(rid == j, o_j, o_rows)
        o = o_rows * lax.rsqrt(jnp.mean(o_rows * o_rows, axis=-1, keepdims=True) + RMS_EPS)
        o_ref[rs, sl] = o * nw_ref[...] * _silu(zg)


def _hgrn_kernel(hq_ref, hf_ref, hi_ref, hg_ref, lb_ref, nw_ref, s_in_ref, o_ref, s_p_ref, s_s_ref, st_ref,
                 *, heads, batch, n_t, n_dec):
    b = pl.program_id(0)
    t = pl.program_id(1)

    @pl.when(jnp.logical_and(b < batch, t < n_t))
    def _():
        _hgrn_prompt_tile(hq_ref, hf_ref, hi_ref, hg_ref, lb_ref, nw_ref, o_ref, s_p_ref, st_ref,
                          t, n_t - 1, heads)

    @pl.when(jnp.logical_and(b == batch, t < n_dec))
    def _():
        _hgrn_decode_rows(hq_ref, hf_ref, hi_ref, hg_ref, lb_ref, nw_ref, s_in_ref, o_ref, s_s_ref, t, heads)


def _hgrn(h_all, lb_rows, norm_w, state, layer, batch, seq):
    n_s, heads = state.shape[1], state.shape[2]
    assert n_s == MIX_TILE, "the sample group must fill exactly one mixer tile"
    width = heads * HG_HEAD_DIM
    n_t = seq // MIX_TILE
    n_dec = n_s // DEC_ROWS
    m = batch * seq + n_s

    def row_blk(b, t):
        return jnp.where(b < batch, b * n_t + jnp.minimum(t, n_t - 1), batch * n_t)

    def dec_blk(b, t):
        return jnp.where(b == batch, jnp.minimum(t, n_dec - 1), 0)

    def hspec(j):
        return pl.BlockSpec((MIX_TILE, width), lambda b, t: (row_blk(b, t), j))

    s_shape = (DEC_ROWS, heads, HG_HEAD_DIM, HG_HEAD_DIM)
    s_in_spec = pl.BlockSpec((None,) + s_shape, lambda b, t: (layer, dec_blk(b, t), 0, 0, 0))
    s_spec = pl.BlockSpec(s_shape, lambda b, t: (dec_blk(b, t), 0, 0, 0))
    return pl.pallas_call(
        functools.partial(_hgrn_kernel, heads=heads, batch=batch, n_t=n_t, n_dec=n_dec),
        grid=(batch + 1, max(n_t, n_dec)),
        in_specs=[hspec(0), hspec(1), hspec(2), hspec(3),
                  pl.BlockSpec((3, width), lambda b, t: (0, 0)),
                  pl.BlockSpec((1, HG_HEAD_DIM), lambda b, t: (0, 0)),
                  s_in_spec],
        out_specs=[pl.BlockSpec((MIX_TILE, width), lambda b, t: (row_blk(b, t), 0)),
                   pl.BlockSpec((1, heads, HG_HEAD_DIM, HG_HEAD_DIM),
                                lambda b, t: (jnp.minimum(b, batch - 1), 0, 0, 0)),
                   s_spec],
        out_shape=[jax.ShapeDtypeStruct((m, width), F32),
                   jax.ShapeDtypeStruct((batch, heads, HG_HEAD_DIM, HG_HEAD_DIM), F32),
                   jax.ShapeDtypeStruct(state.shape[1:], F32)],
        scratch_shapes=[pltpu.VMEM((heads, HG_HEAD_DIM, HG_HEAD_DIM), F32)],
        compiler_params=pltpu.CompilerParams(dimension_semantics=("arbitrary", "arbitrary")),
        name="hgrn",
    )(h_all, h_all, h_all, h_all, lb_rows, norm_w, state)


def _rglru_terms(xc, gate_z, wa_ref, ba_ref, wx_ref, bx_ref, lam_ref):
    r = _sigmoid(_dot(xc, wa_ref[...]) + ba_ref[...])
    i = _sigmoid(_dot(xc, wx_ref[...]) + bx_ref[...])
    lam = lam_ref[...]
    log_sig_lam = jnp.minimum(lam, 0.0) - jnp.log1p(jnp.exp(-jnp.abs(lam)))
    log_a = RG_C * r * log_sig_lam
    a = jnp.exp(log_a)
    one_m_a2 = -jnp.tanh(log_a) * (a * a + 1.0)
    bterm = jnp.sqrt(one_m_a2) * (i * xc)
    return a, bterm, _gelu_tanh(gate_z)


def _rg_pool_prompt_tile(rx_ref, rgate_ref, pin_ref, w, o_rg_ref, o_pl_ref, h_out_ref, ctail_ref, ptail_ref,
                         rx_ext, pin_ext, h_carry, t_idx):
    cw_ref, cb_ref, wa_ref, ba_ref, wx_ref, bx_ref, lam_ref, pw_ref, ps_ref = w
    tile = MIX_TILE
    width = rx_ref.shape[1]
    conv_pad = SUBLANES
    pool_pad = 2 * SUBLANES

    @pl.when(t_idx == 0)
    def _():
        rx_ext[0:conv_pad, :] = jnp.zeros((conv_pad, width), F32)
        pin_ext[0:pool_pad, :] = jnp.zeros((pool_pad, width), F32)
        h_carry[...] = jnp.zeros_like(h_carry)

    x = rx_ref[...]
    rx_ext[conv_pad:conv_pad + tile, :] = x
    n_taps = cw_ref.shape[0]
    xc = x * cw_ref[n_taps - 1:n_taps, :] + cb_ref[...]
    for j in range(1, n_taps):
        xc = xc + rx_ext[conv_pad - j:conv_pad - j + tile, :] * cw_ref[n_taps - 1 - j:n_taps - j, :]
    tail = rx_ext[tile:tile + conv_pad, :]
    rx_ext[0:conv_pad, :] = tail
    ctail_ref[0] = tail

    a, bterm, gate = _rglru_terms(xc, rgate_ref[...], wa_ref, ba_ref, wx_ref, bx_ref, lam_ref)

    rid = lax.broadcasted_iota(I32, (tile, LANES), 0)
    for blk in range(width // LANES):
        sl = slice(blk * LANES, (blk + 1) * LANES)
        a_c, b_c = a[:, sl], bterm[:, sl]
        s = 1
        while s < tile:
            a_sh = jnp.where(rid >= s, pltpu.roll(a_c, s, axis=0), 1.0)
            b_sh = jnp.where(rid >= s, pltpu.roll(b_c, s, axis=0), 0.0)
            b_c = a_c * b_sh + b_c
            a_c = a_c * a_sh
            s *= 2
        hh = a_c * h_carry[:, sl] + b_c
        h_carry[:, sl] = hh[tile - 1:tile, :]
        o_rg_ref[:, sl] = hh * gate[:, sl]

    h_out_ref[0] = h_carry[...]

    p = pin_ref[...]
    pin_ext[pool_pad:pool_pad + tile, :] = p
    pos1 = t_idx * tile + rid + 1
    group = width // len(POOL_WINDOWS)
    pooled = []
    for g, win_len in enumerate(POOL_WINDOWS):
        sl = slice(g * group, (g + 1) * group)
        win = p[:, sl]
        for j in range(1, win_len):
            win = win + pin_ext[pool_pad - j:pool_pad - j + tile, sl]
        count = jnp.minimum(pos1, win_len).astype(F32)
        pooled.append(win / count - p[:, sl])
    ptail = pin_ext[tile:tile + pool_pad, :]
    pin_ext[0:pool_pad, :] = ptail
    ptail_ref[0] = ptail
    o_pl_ref[...] = _dot(jnp.concatenate(pooled, axis=1), pw_ref[...]) * ps_ref[...]


def _rg_pool_decode_rows(rx_ref, rgate_ref, pin_ref, cbuf_ref, h0_ref, pbuf_ref, w, o_rg_ref, o_pl_ref, h_out_ref):
    cw_ref, cb_ref, wa_ref, ba_ref, wx_ref, bx_ref, lam_ref, pw_ref, ps_ref = w
    x = rx_ref[...]
    n_taps = cw_ref.shape[0]
    xc = x * cw_ref[n_taps - 1:n_taps, :] + cb_ref[...]
    for j in range(n_taps - 1):
        xc = xc + cbuf_ref[j] * cw_ref[j:j + 1, :]
    a, bterm, gate = _rglru_terms(xc, rgate_ref[...], wa_ref, ba_ref, wx_ref, bx_ref, lam_ref)
    hh = a * h0_ref[...] + bterm
    h_out_ref[...] = hh
    o_rg_ref[...] = hh * gate

    p = pin_ref[...]
    n_buf = pbuf_ref.shape[0]
    width = p.shape[1]
    group = width // len(POOL_WINDOWS)
    pooled = []
    for g, win_len in enumerate(POOL_WINDOWS):
        sl = slice(g * group, (g + 1) * group)
        win = p[:, sl]
        for j in range(1, win_len):
            win = win + pbuf_ref[n_buf - j, :, sl]
        pooled.append(win / float(min(PAST_LEN + 1, win_len)) - p[:, sl])
    o_pl_ref[...] = _dot(jnp.concatenate(pooled, axis=1), pw_ref[...]) * ps_ref[...]


def _rg_pool_kernel(rx_ref, rgate_ref, pin_ref, cbuf_ref, h0_ref, pbuf_ref,
                    cw_ref, cb_ref, wa_ref, ba_ref, wx_ref, bx_ref, lam_ref, pw_ref, ps_ref,
                    o_rg_ref, o_pl_ref, h_p_ref, ctail_ref, ptail_ref, h_s_ref,
                    rx_ext, pin_ext, h_carry, *, batch):
    b = pl.program_id(0)
    t = pl.program_id(1)
    w = (cw_ref, cb_ref, wa_ref, ba_ref, wx_ref, bx_ref, lam_ref, pw_ref, ps_ref)

    @pl.when(b < batch)
    def _():
        _rg_pool_prompt_tile(rx_ref, rgate_ref, pin_ref, w, o_rg_ref, o_pl_ref, h_p_ref, ctail_ref, ptail_ref,
                             rx_ext, pin_ext, h_carry, t)

    @pl.when(jnp.logical_and(b == batch, t == 0))
    def _():
        _rg_pool_decode_rows(rx_ref, rgate_ref, pin_ref, cbuf_ref, h0_ref, pbuf_ref, w, o_rg_ref, o_pl_ref, h_s_ref)


def _rg_pool(h_all, cbuf_t, h0, pbuf_t, rg_w, pool_w, batch, seq, col0):
    n_s, width = h0.shape
    assert n_s == MIX_TILE, "the sample group must fill exactly one mixer tile"
    n_t = seq // MIX_TILE
    c0 = col0 // width
    m = batch * seq + n_s

    def row_blk(b, t):
        return jnp.where(b < batch, b * n_t + t, batch * n_t)

    def prompt_blk(b):
        return jnp.minimum(b, batch - 1)

    def hspec(j):
        return pl.BlockSpec((MIX_TILE, width), lambda b, t: (row_blk(b, t), c0 + j))

    def full(arr):
        return pl.BlockSpec(arr.shape, lambda b, t: (0,) * arr.ndim)

    weights = [rg_w["conv_w"], rg_w["conv_b"], rg_w["wa"], rg_w["ba"], rg_w["wx"], rg_w["bx"], rg_w["lam"],
               pool_w["w"], pool_w["scale"]]
    o_spec = pl.BlockSpec((MIX_TILE, width), lambda b, t: (row_blk(b, t), 0))
    return pl.pallas_call(
        functools.partial(_rg_pool_kernel, batch=batch),
        grid=(batch + 1, n_t),
        in_specs=[hspec(0), hspec(1), hspec(2), full(cbuf_t), full(h0), full(pbuf_t)] + [full(w) for w in weights],
        out_specs=[o_spec, o_spec,
                   pl.BlockSpec((1, 1, width), lambda b, t: (prompt_blk(b), 0, 0)),
                   pl.BlockSpec((1, SUBLANES, width), lambda b, t: (prompt_blk(b), 0, 0)),
                   pl.BlockSpec((1, 2 * SUBLANES, width), lambda b, t: (prompt_blk(b), 0, 0)),
                   pl.BlockSpec((n_s, width), lambda b, t: (0, 0))],
        out_shape=[jax.ShapeDtypeStruct((m, width), F32), jax.ShapeDtypeStruct((m, width), F32),
                   jax.ShapeDtypeStruct((batch, 1, width), F32),
                   jax.ShapeDtypeStruct((batch, SUBLANES, width), F32),
                   jax.ShapeDtypeStruct((batch, 2 * SUBLANES, width), F32),
                   jax.ShapeDtypeStruct((n_s, width), F32)],
        scratch_shapes=[pltpu.VMEM((MIX_TILE + SUBLANES, width), F32),
                        pltpu.VMEM((MIX_TILE + 2 * SUBLANES, width), F32),
                        pltpu.VMEM((1, width), F32)],
        compiler_params=pltpu.CompilerParams(dimension_semantics=("arbitrary", "arbitrary")),
        name="rg_pool",
    )(h_all, h_all, h_all, cbuf_t, h0, pbuf_t, *weights)


def _post_mix_kernel(x_ref, o_hg_ref, o_rg_ref, o_pl_ref, w_ref, g_ref, b_ref, rwh_ref, rwl_ref, rb_ref,
                     x1_ref, idx_ref, gate_ref, *, alpha, n_experts):
    n_hg = o_hg_ref.shape[1]
    n_rg = o_rg_ref.shape[1]
    y = _dot(o_hg_ref[...], w_ref[0:n_hg, :])
    y = y + _dot(o_rg_ref[...], w_ref[n_hg:n_hg + n_rg, :])
    y = y + _dot(o_pl_ref[...], w_ref[n_hg + n_rg:, :])
    x1 = _layer_norm(alpha * x_ref[...] + y, g_ref[...], b_ref[...])
    x1_ref[...] = x1

    x_hi = x1.astype(BF16)
    x_lo = (x1 - x_hi.astype(F32)).astype(BF16)
    logits = (jnp.dot(x_hi, rwh_ref[...], preferred_element_type=F32)
              + jnp.dot(x_lo, rwh_ref[...], preferred_element_type=F32)
              + jnp.dot(x_hi, rwl_ref[...], preferred_element_type=F32)) + rb_ref[...]

    lane = lax.broadcasted_iota(I32, logits.shape, 1).astype(F32)
    vals = jnp.where(lane < n_experts, logits, NEG_BIG)
    idx_out = jnp.zeros(logits.shape, F32)
    exp_out = jnp.zeros(logits.shape, F32)
    v0 = None
    denom = None
    for k in range(TOP_K):
        m = jnp.max(vals, axis=-1, keepdims=True)
        idx = jnp.min(jnp.where(vals == m, lane, float(LANES)), axis=-1, keepdims=True)
        if k == 0:
            v0 = m
        e = jnp.exp(m - v0)
        denom = e if denom is None else denom + e
        idx_out = jnp.where(lane == k, idx, idx_out)
        exp_out = jnp.where(lane == k, e, exp_out)
        vals = jnp.where(lane == idx, 2.0 * NEG_BIG, vals)
    idx_ref[...] = idx_out.astype(I32)
    gate_ref[...] = exp_out / denom


def _post_mix(x, o_hg, o_rg, o_pl, w_out_bf16, layer, ln_g, ln_b, rw_hi, rw_lo, rb_pad, alpha, n_experts):
    m, d = x.shape

    def rows(arr):
        return pl.BlockSpec((POST_TM, arr.shape[1]), lambda i: (i, 0))

    def full(arr):
        return pl.BlockSpec(arr.shape, lambda i: (0,) * arr.ndim)

    w_spec = pl.BlockSpec((None,) + w_out_bf16.shape[1:], lambda i: (layer, 0, 0))
    return pl.pallas_call(
        functools.partial(_post_mix_kernel, alpha=alpha, n_experts=n_experts),
        grid=(m // POST_TM,),
        in_specs=[rows(x), rows(o_hg), rows(o_rg), rows(o_pl), w_spec, full(ln_g), full(ln_b),
                  full(rw_hi), full(rw_lo), full(rb_pad)],
        out_specs=[pl.BlockSpec((POST_TM, d), lambda i: (i, 0)),
                   pl.BlockSpec((POST_TM, LANES), lambda i: (i, 0)),
                   pl.BlockSpec((POST_TM, LANES), lambda i: (i, 0))],
        out_shape=[jax.ShapeDtypeStruct((m, d), F32), jax.ShapeDtypeStruct((m, LANES), I32),
                   jax.ShapeDtypeStruct((m, LANES), F32)],
        compiler_params=pltpu.CompilerParams(dimension_semantics=("parallel",),
                                             vmem_limit_bytes=VMEM_BYTES_V7X * 3 // 4),
        name="post_mix",
    )(x, o_hg, o_rg, o_pl, w_out_bf16, ln_g, ln_b, rw_hi, rw_lo, rb_pad)


def _for_rows(lo, hi, fn):
    g_lo = lo >> SUBLANE_SHIFT
    g_hi = hi >> SUBLANE_SHIFT

    def group_body(g, carry):
        for u in range(SUBLANES):
            fn(g, u)
        return carry

    def single_body(r, carry):
        fn(g_hi, r - (g_hi << SUBLANE_SHIFT))
        return carry

    lax.fori_loop(g_lo, g_hi, group_body, 0)
    lax.fori_loop(g_hi << SUBLANE_SHIFT, hi, single_body, 0)


def _moe_kernel(item_expert, item_rows, item_start, order_hbm, x_hbm, wgu_ref, bgu_ref, wd_ref, bd_ref, y_hbm,
                acc_ref, stage_ref, xg_ref, wgu_bf, wd_bf, idx_ref, sem_idx, sem_in, sem_out, *, n_tokens):
    del item_expert
    w = pl.program_id(0)
    c = pl.program_id(1)
    n_items = pl.num_programs(0)
    n_chunks = pl.num_programs(1)
    last_c = n_chunks - 1
    n = item_rows[w]
    n_blk = (n + MOE_BLK - 1) // MOE_BLK
    d = xg_ref.shape[1]
    tn = wgu_bf.shape[1]
    blk_groups = MOE_BLK // SUBLANES
    nxt = jnp.minimum(w + 1, n_items - 1)
    n_next = jnp.where(w + 1 < n_items, item_rows[nxt], 0)
    this_win = pl.multiple_of((w & 1) * MOE_IDX_LEN, SMEM_1D_TILE)
    next_win = pl.multiple_of((1 - (w & 1)) * MOE_IDX_LEN, SMEM_1D_TILE)

    def window_offset(item):
        return item_start[item] & (SMEM_1D_TILE - 1)

    def window_copy(item, win):
        src0 = pl.multiple_of(item_start[item] - window_offset(item), SMEM_1D_TILE)
        return pltpu.make_async_copy(order_hbm.at[pl.ds(src0, MOE_IDX_LEN)], idx_ref.at[pl.ds(win, MOE_IDX_LEN)],
                                     sem_idx)

    def start_gather(item, win, count):
        first = win + window_offset(item)

        def issue(g, u):
            pair = idx_ref[first + (g << SUBLANE_SHIFT) + u]
            pltpu.make_async_copy(x_hbm.at[pl.ds(pair >> TOP_K_SHIFT, 1)], stage_ref.at[g, pl.ds(u, 1)],
                                  sem_in).start()
        _for_rows(0, count, issue)

    def start_scatter(lo, hi):
        first = this_win + window_offset(w)

        def issue(g, u):
            pair = idx_ref[first + (g << SUBLANE_SHIFT) + u]
            dst = (pair & (TOP_K - 1)) * n_tokens + (pair >> TOP_K_SHIFT)
            pltpu.make_async_copy(acc_ref.at[g, pl.ds(u, 1)], y_hbm.at[pl.ds(dst, 1)], sem_out).start()
        _for_rows(lo, hi, issue)

    def wait_gather(count):
        _for_rows(0, count, lambda g, u: pltpu.make_async_copy(
            x_hbm.at[pl.ds(0, 1)], stage_ref.at[0, pl.ds(0, 1)], sem_in).wait())

    def wait_scatter(count):
        _for_rows(0, count, lambda g, u: pltpu.make_async_copy(
            acc_ref.at[0, pl.ds(0, 1)], y_hbm.at[pl.ds(0, 1)], sem_out).wait())

    @pl.when(jnp.logical_and(w == 0, c == 0))
    def _():
        stage_ref[...] = jnp.zeros_like(stage_ref)

        @pl.when(n > 0)
        def _():
            window_copy(0, this_win).start()
            window_copy(0, this_win).wait()
            start_gather(0, this_win, n)

    @pl.when(n > 0)
    def _():
        @pl.when(c == 0)
        def _():
            wait_gather(n)

            @pl.when(w > 0)
            def _():
                wait_scatter(item_rows[jnp.maximum(w - 1, 0)])

            def cast_body(i, carry):
                gs = pl.ds(pl.multiple_of(i * blk_groups, blk_groups), blk_groups)
                rs = pl.ds(pl.multiple_of(i * MOE_BLK, MOE_BLK), MOE_BLK)
                xg_ref[rs, :] = stage_ref[gs].reshape(MOE_BLK, d).astype(BF16)
                acc_ref[gs] = jnp.broadcast_to(bd_ref[...], (blk_groups, SUBLANES, d))
                return carry
            lax.fori_loop(0, n_blk, cast_body, 0)

            @pl.when(n_next > 0)
            def _():
                window_copy(nxt, next_win).start()

        @pl.when(jnp.logical_and(c == jnp.minimum(1, last_c), n_next > 0))
        def _():
            window_copy(nxt, next_win).wait()
            start_gather(nxt, next_win, n_next)

        wgu_bf[...] = wgu_ref[...].astype(BF16)
        wd_bf[...] = wd_ref[...].astype(BF16)
        sel_r = lax.broadcasted_iota(I32, (tn, tn // 2), 0)
        sel_c = lax.broadcasted_iota(I32, (tn, tn // 2), 1)
        pick = jnp.where(sel_r == 2 * sel_c, 1.0, 0.0).astype(BF16)

        def sub_tile(row0, rows):
            groups = rows // SUBLANES
            gs = pl.ds(pl.multiple_of(row0 >> SUBLANE_SHIFT, blk_groups), groups)
            h = jnp.dot(xg_ref[pl.ds(row0, rows), :], wgu_bf[...], preferred_element_type=F32) + bgu_ref[...]
            h_next = pltpu.roll(h, tn - 1, axis=1)
            gate = jnp.minimum(h, SWIGLU_LIMIT)
            up = jnp.clip(h_next, -SWIGLU_LIMIT, SWIGLU_LIMIT)
            act = (up + 1.0) * gate * _sigmoid(gate * SWIGLU_ALPHA)
            act_dense = jnp.dot(act.astype(BF16), pick, preferred_element_type=F32)
            upd = jnp.dot(act_dense.astype(BF16), wd_bf[...], preferred_element_type=F32)
            acc_ref[gs] = acc_ref[gs] + upd.reshape(groups, SUBLANES, d)

            @pl.when(c == last_c)
            def _():
                start_scatter(row0, jnp.minimum(row0 + rows, n))

        big = MOE_SUB_ROWS[0]
        n_big = n_blk // (big // MOE_BLK)

        def big_body(i, carry):
            sub_tile(pl.multiple_of(i * big, big), big)
            return carry
        lax.fori_loop(0, n_big, big_body, 0)
        row = n_big * big
        for rows in MOE_SUB_ROWS[1:]:
            take = (n_blk & (rows // MOE_BLK)) != 0

            @pl.when(take)
            def _(row=row, rows=rows):
                sub_tile(pl.multiple_of(row, MOE_BLK), rows)
            row = row + jnp.where(take, rows, 0)

        @pl.when(jnp.logical_and(c == last_c, n_next == 0))
        def _():
            wait_scatter(n)


def _moe(x1, tables, wgu, bgu, wd, bd, layer, n_items):
    m, d = x1.shape
    depth, n_exp, _, n_gu = wgu.shape
    n_chunks = n_gu // MOE_TN
    half = MOE_TN // 2
    row_groups = MOE_TM // SUBLANES

    def chunk(c, rows_ref, w):
        return jnp.where(rows_ref[w] > 0, c, n_chunks - 1)

    grid_spec = pltpu.PrefetchScalarGridSpec(
        num_scalar_prefetch=3,
        grid=(n_items, n_chunks),
        in_specs=[
            pl.BlockSpec(memory_space=pl.ANY),
            pl.BlockSpec(memory_space=pl.ANY),
            pl.BlockSpec((None, None, d, MOE_TN), lambda w, c, ie, ir, ist: (layer, ie[w], 0, chunk(c, ir, w))),
            pl.BlockSpec((None, None, 1, MOE_TN), lambda w, c, ie, ir, ist: (layer, ie[w], 0, chunk(c, ir, w))),
            pl.BlockSpec((None, None, half, d), lambda w, c, ie, ir, ist: (layer, ie[w], chunk(c, ir, w), 0)),
            pl.BlockSpec((None, None, 1, d), lambda w, c, ie, ir, ist: (layer, ie[w], 0, 0)),
        ],
        out_specs=pl.BlockSpec(memory_space=pl.ANY),
        scratch_shapes=[pltpu.VMEM((row_groups, SUBLANES, d), F32), pltpu.VMEM((row_groups, SUBLANES, d), F32),
                        pltpu.VMEM((MOE_TM, d), BF16),
                        pltpu.VMEM((d, MOE_TN), BF16), pltpu.VMEM((half, d), BF16),
                        pltpu.SMEM((2 * MOE_IDX_LEN,), I32),
                        pltpu.SemaphoreType.DMA, pltpu.SemaphoreType.DMA, pltpu.SemaphoreType.DMA],
    )
    return pl.pallas_call(
        functools.partial(_moe_kernel, n_tokens=m),
        grid_spec=grid_spec,
        out_shape=jax.ShapeDtypeStruct((TOP_K * m, d), F32),
        compiler_params=pltpu.CompilerParams(dimension_semantics=("arbitrary", "arbitrary"),
                                             vmem_limit_bytes=VMEM_BYTES_V7X * 7 // 8),
        name="moe_experts",
    )(tables["expert"], tables["rows"], tables["start"], tables["order"], x1,
      wgu, bgu.reshape(depth, n_exp, 1, n_gu), wd, bd.reshape(depth, n_exp, 1, d))


def _route_tables(top_idx, n_experts, n_items):
    m = top_idx.shape[0]
    n_pairs = m * TOP_K
    pair_expert = top_idx.reshape(-1)
    order = jnp.argsort(pair_expert, stable=True).astype(I32)
    counts = jnp.sum((pair_expert[:, None] == jnp.arange(n_experts, dtype=I32)[None, :]).astype(I32), axis=0)
    row_end = jnp.cumsum(counts)
    row_off = row_end - counts
    tiles = (counts + MOE_TM - 1) // MOE_TM
    tile_end = jnp.cumsum(tiles)
    tile_off = tile_end - tiles
    total = tile_end[-1]
    w = jnp.arange(n_items, dtype=I32)
    e_w = jnp.minimum(jnp.searchsorted(tile_end, w, side="right").astype(I32), n_experts - 1)
    j = w - tile_off[e_w]
    active = w < total
    start = jnp.where(active, row_off[e_w] + j * MOE_TM, 0)
    rows = jnp.where(active, jnp.clip(counts[e_w] - j * MOE_TM, 0, MOE_TM), 0)
    e_w = jnp.where(active, e_w, e_w[jnp.maximum(total - 1, 0)])
    padded = -(-n_pairs // SMEM_1D_TILE) * SMEM_1D_TILE + MOE_IDX_LEN
    return {"expert": e_w.astype(I32), "rows": rows.astype(I32), "start": start.astype(I32),
            "order": jnp.pad(order, (0, padded - n_pairs))}


def _combine_kernel(x1_ref, gate_ref, y_ref, g_ref, b_ref, o_ref, *, alpha):
    gates = gate_ref[...]
    acc = alpha * x1_ref[...]
    for k in range(TOP_K):
        acc = acc + gates[:, k:k + 1] * y_ref[k]
    o_ref[...] = _layer_norm(acc, g_ref[...], b_ref[...])


def _combine(x1, gates, y, ln_g, ln_b, alpha):
    m, d = x1.shape
    return pl.pallas_call(
        functools.partial(_combine_kernel, alpha=alpha),
        grid=(m // COMBINE_TM,),
        in_specs=[pl.BlockSpec((COMBINE_TM, d), lambda i: (i, 0)),
                  pl.BlockSpec((COMBINE_TM, LANES), lambda i: (i, 0)),
                  pl.BlockSpec((TOP_K, COMBINE_TM, d), lambda i: (0, i, 0)),
                  pl.BlockSpec((1, d), lambda i: (0, 0)),
                  pl.BlockSpec((1, d), lambda i: (0, 0))],
        out_specs=pl.BlockSpec((COMBINE_TM, d), lambda i: (i, 0)),
        out_shape=jax.ShapeDtypeStruct((m, d), F32),
        compiler_params=pltpu.CompilerParams(dimension_semantics=("parallel",),
                                             vmem_limit_bytes=VMEM_BYTES_V7X * 3 // 4),
        name="moe_combine",
    )(x1, gates, y.reshape(TOP_K, m, d), ln_g, ln_b)


def _block_diag(w):
    nb, n, _ = w.shape
    eye = jnp.eye(nb, dtype=w.dtype)
    return (eye[:, None, :, None] * w[:, :, None, :]).reshape(nb * n, nb * n)


def _row(v):
    return v.reshape(1, -1).astype(F32)


def kernel(x_prompt, x_sample, state_hgrn, state_rglru_h, state_rglru_conv, state_pool, w_in, hg_lb_logits,
           hg_norm_w, rg_conv_w, rg_conv_b, rg_wa, rg_ba, rg_wx, rg_bx, rg_lambda, pool_w, pool_scale, w_out,
           ln1_g, ln1_b, router_w, router_b, moe_w_gate_up, moe_b_gate_up, moe_w_down, moe_b_down, ln2_g, ln2_b):
    depth = w_in.shape[0]
    bp, lp, d_model = x_prompt.shape
    bs = x_sample.shape[0]
    n_p = bp * lp
    m = n_p + bs
    hg_width = hg_lb_logits.shape[1]
    rg_width = rg_lambda.shape[1]
    n_experts = router_w.shape[2]
    n_conv = state_rglru_conv.shape[2]
    n_pool = state_pool.shape[2]
    alpha = float((2 * depth) ** 0.25)
    n_items = n_experts + (m * TOP_K) // MOE_TM
    rg_col0 = 4 * hg_width

    p_lb = jax.nn.softmax(hg_lb_logits.astype(F32), axis=0)
    lbs = jnp.cumsum(p_lb, axis=0)
    lbs = lbs - lbs[0]

    w_in_bf16 = w_in.astype(BF16)
    w_out_bf16 = w_out.astype(BF16)
    x = jnp.concatenate([x_prompt.reshape(n_p, d_model), x_sample.reshape(bs, d_model)], axis=0)
    outs = {k: [] for k in ("hg_p", "hg_s", "h_p", "h_s", "c_p", "c_s", "pl_p", "pl_s")}
    for l in range(depth):
        lb = lbs[l]
        lb_rows = jnp.stack([jnp.log(lb), jnp.log1p(-lb), 1.0 - lb], axis=0)
        rg_w = {"conv_w": rg_conv_w[l], "conv_b": _row(rg_conv_b[l]), "wa": _block_diag(rg_wa[l]).astype(BF16),
                "ba": _row(rg_ba[l]), "wx": _block_diag(rg_wx[l]).astype(BF16), "bx": _row(rg_bx[l]),
                "lam": _row(rg_lambda[l])}
        pl_w = {"w": _block_diag(pool_w[l]).astype(BF16), "scale": _row(pool_scale[l])}

        h_all = _in_proj(x, w_in_bf16, l)
        o_hg, s_p, s_s = _hgrn(h_all, lb_rows, _row(hg_norm_w[l]), state_hgrn, l, bp, lp)
        o_rg, o_pl, h_p, c_tail, p_tail, h_s = _rg_pool(
            h_all, jnp.swapaxes(state_rglru_conv[l], 0, 1), state_rglru_h[l], jnp.swapaxes(state_pool[l], 0, 1),
            rg_w, pl_w, bp, lp, rg_col0)

        rx_s = h_all[n_p:, rg_col0:rg_col0 + rg_width]
        pin_s = h_all[n_p:, rg_col0 + 2 * rg_width:]
        outs["hg_p"].append(s_p)
        outs["hg_s"].append(s_s)
        outs["h_p"].append(h_p.reshape(bp, rg_width))
        outs["h_s"].append(h_s)
        outs["c_p"].append(c_tail[:, SUBLANES - n_conv:])
        outs["c_s"].append(jnp.concatenate([state_rglru_conv[l][:, 1:], rx_s[:, None, :]], axis=1))
        outs["pl_p"].append(p_tail[:, 2 * SUBLANES - n_pool:])
        outs["pl_s"].append(jnp.concatenate([state_pool[l][:, 1:], pin_s[:, None, :]], axis=1))

        rw = jnp.pad(router_w[l].astype(F32), ((0, 0), (0, LANES - n_experts)))
        rw_hi = rw.astype(BF16)
        rw_lo = (rw - rw_hi.astype(F32)).astype(BF16)
        rb_pad = jnp.pad(router_b[l].astype(F32), (0, LANES - n_experts)).reshape(1, LANES)
        x1, top_idx, gates = _post_mix(x, o_hg, o_rg, o_pl, w_out_bf16, l, _row(ln1_g[l]), _row(ln1_b[l]),
                                       rw_hi, rw_lo, rb_pad, alpha, n_experts)

        tables = _route_tables(top_idx[:, :TOP_K], n_experts, n_items)
        y = _moe(x1, tables, moe_w_gate_up, moe_b_gate_up, moe_w_down, moe_b_down, l, n_items)
        x = _combine(x1, gates, y, _row(ln2_g[l]), _row(ln2_b[l]), alpha)

    return (x[:n_p].reshape(bp, lp, d_model), x[n_p:].reshape(bs, 1, d_model),
            jnp.stack(outs["hg_p"]), jnp.stack(outs["hg_s"]), jnp.stack(outs["h_p"]), jnp.stack(outs["h_s"]),
            jnp.stack(outs["c_p"]), jnp.stack(outs["c_s"]), jnp.stack(outs["pl_p"]), jnp.stack(outs["pl_s"]))
```

```python
import functools

import jax
import jax.numpy as jnp
from jax import lax
from jax.experimental import pallas as pl
from jax.experimental.pallas import tpu as pltpu

F32 = jnp.float32
BF16 = jnp.bfloat16
I32 = jnp.int32

HG_HEAD_DIM = 128
RG_C = 8.0
POOL_WINDOWS = (2, 4, 8, 16)
TOP_K = 4
TOP_K_SHIFT = TOP_K.bit_length() - 1
SWIGLU_LIMIT = 7.0
SWIGLU_ALPHA = 1.702
LN_EPS = 1e-5
RMS_EPS = 1e-6
PAST_LEN = 16384
GELU_C0 = 0.7978845608028654
GELU_C1 = 0.044715

LANES = 128
SUBLANES = 8
SUBLANE_SHIFT = SUBLANES.bit_length() - 1
VMEM_BYTES_V7X = 64 * 1024 * 1024
SMEM_1D_TILE = 1024

PROJ_TM = 640
PROJ_TN = 1408
POST_TM = 320
MIX_TILE = 128
DEC_ROWS = 8
MOE_TM = 1536
MOE_BLK = 128
MOE_SUB_ROWS = (512, 256, 128)
MOE_TN = 512
MOE_IDX_LEN = -(-(MOE_TM + SMEM_1D_TILE - 1) // SMEM_1D_TILE) * SMEM_1D_TILE
COMBINE_TM = 320
NEG_BIG = -1e30

assert TOP_K == 1 << TOP_K_SHIFT


def _sigmoid(x):
    return jax.nn.sigmoid(x)


def _silu(x):
    return x * _sigmoid(x)


def _gelu_tanh(x):
    return 0.5 * x * (1.0 + jnp.tanh(GELU_C0 * (x + GELU_C1 * (x * x * x))))


def _dot(a, b):
    return jnp.dot(a.astype(BF16), b.astype(BF16), preferred_element_type=F32)


def _dot_nt(a, b):
    return lax.dot_general(a.astype(BF16), b.astype(BF16), (((1,), (1,)), ((), ())),
                           preferred_element_type=F32)


def _dot_tn(a, b):
    return lax.dot_general(a.astype(BF16), b.astype(BF16), (((0,), (0,)), ((), ())),
                           preferred_element_type=F32)


def _layer_norm(x, g, b):
    mu = jnp.mean(x, axis=-1, keepdims=True)
    xc = x - mu
    var = jnp.mean(xc * xc, axis=-1, keepdims=True)
    return xc * lax.rsqrt(var + LN_EPS) * g + b


def _in_proj_kernel(x_ref, w_ref, o_ref):
    o_ref[...] = jnp.dot(x_ref[...].astype(BF16), w_ref[...], preferred_element_type=F32)


def _in_proj(x, w_bf16, layer):
    m, k = x.shape
    n = w_bf16.shape[2]
    return pl.pallas_call(
        _in_proj_kernel,
        grid=(m // PROJ_TM, n // PROJ_TN),
        in_specs=[pl.BlockSpec((PROJ_TM, k), lambda i, j: (i, 0)),
                  pl.BlockSpec((None, k, PROJ_TN), lambda i, j: (layer, 0, j))],
        out_specs=pl.BlockSpec((PROJ_TM, PROJ_TN), lambda i, j: (i, j)),
        out_shape=jax.ShapeDtypeStruct((m, n), F32),
        compiler_params=pltpu.CompilerParams(dimension_semantics=("parallel", "arbitrary"),
                                             vmem_limit_bytes=VMEM_BYTES_V7X * 3 // 4),
        name="in_proj",
    )(x, w_bf16)


def _hgrn_gates(z, log_lb, log1m_lb, one_m_lb):
    e = jnp.exp(-jnp.abs(z))
    log_sig = jnp.minimum(z, 0.0) - jnp.log1p(e)
    c = log1m_lb + log_sig
    logf = jnp.maximum(log_lb, c) + jnp.log1p(jnp.exp(-jnp.abs(log_lb - c)))
    k = one_m_lb * (jnp.where(z > 0, e, 1.0) / (1.0 + e))
    return logf, k


def _cumsum_rows(x, rid):
    n = x.shape[0]
    s = 1
    while s < n:
        x = x + jnp.where(rid >= s, pltpu.roll(x, s, axis=0), 0.0)
        s *= 2
    return x


def _hgrn_prompt_tile(hq_ref, hf_ref, hi_ref, hg_ref, lb_ref, nw_ref, o_ref, s_out_ref, st_ref,
                      t_idx, last_t, heads):
    tile, hd = MIX_TILE, HG_HEAD_DIM
    half, quarter, mid = tile // 2, tile // 4, tile // 8

    @pl.when(t_idx == 0)
    def _():
        st_ref[...] = jnp.zeros_like(st_ref)

    row = lax.broadcasted_iota(I32, (tile, tile), 0)
    col = lax.broadcasted_iota(I32, (tile, tile), 1)
    rid = lax.broadcasted_iota(I32, (tile, hd), 0)
    mask_a = (row >= half) & (col < half)
    q_shift = quarter.bit_length() - 1
    row_q, col_q = row >> q_shift, col >> q_shift
    mask_b = (row_q == col_q + 1) & ((col_q & 1) == 0)
    mask_d = (row_q == col_q) & (col <= row)

    def head_body(h):
        sl = slice(h * hd, (h + 1) * hd)
        zq = hq_ref[:, sl]
        zf = hf_ref[:, sl]
        v = hi_ref[:, sl]
        zg = hg_ref[:, sl]
        q = _silu(zq)
        logf, k = _hgrn_gates(zf, lb_ref[0:1, sl], lb_ref[1:2, sl], lb_ref[2:3, sl])
        b = _cumsum_rows(logf, rid)

        def split_level(ref_rows):
            d = b - ref_rows
            e = jnp.exp(-jnp.abs(d))
            return q * jnp.where(d < 0, e, 1.0), k * jnp.where(d < 0, 1.0, e)

        qa, ka = split_level(b[half - 1:half, :])
        ref_b = jnp.where(rid < half, b[quarter - 1:quarter, :], b[half + quarter - 1:half + quarter, :])
        qb, kb = split_level(ref_b)
        ref_d = b[mid - 1:mid, :]
        for blk in range(1, 4):
            lo = blk * quarter
            ref_d = jnp.where(rid >= lo, b[lo + mid - 1:lo + mid, :], ref_d)
        dd = b - ref_d
        qd = q * jnp.exp(dd)
        kd = k * jnp.exp(-dd)

        scores = jnp.where(mask_a, _dot_nt(qa, ka),
                           jnp.where(mask_b, _dot_nt(qb, kb),
                                     jnp.where(mask_d, _dot_nt(qd, kd), 0.0)))
        st = st_ref[h]
        o = _dot(scores, v) + _dot_nt(q * jnp.exp(b), st)
        b_last = b[tile - 1:tile, :]
        st_new = st * jnp.exp(b_last) + _dot_tn(v, k * jnp.exp(b_last - b))
        st_ref[h] = st_new

        o = o * lax.rsqrt(jnp.mean(o * o, axis=-1, keepdims=True) + RMS_EPS) * nw_ref[...] * _silu(zg)
        o_ref[:, sl] = o

    for h in range(heads):
        head_body(h)

    @pl.when(t_idx == last_t)
    def _():
        for h in range(heads):
            s_out_ref[0, h] = st_ref[h].T


def _hgrn_decode_rows(hq_ref, hf_ref, hi_ref, hg_ref, lb_ref, nw_ref, s_ref, o_ref, s_out_ref, step, heads):
    hd = HG_HEAD_DIM
    rows = DEC_ROWS
    rid = lax.broadcasted_iota(I32, (rows, hd), 0)
    rs = pl.ds(pl.multiple_of(step * rows, rows), rows)

    for h in range(heads):
        sl = slice(h * hd, (h + 1) * hd)
        zq = hq_ref[rs, sl]
        zf = hf_ref[rs, sl]
        v = hi_ref[rs, sl]
        zg = hg_ref[rs, sl]
        q = _silu(zq)
        logf, k = _hgrn_gates(zf, lb_ref[0:1, sl], lb_ref[1:2, sl], lb_ref[2:3, sl])
        f = jnp.exp(logf)
        stack = jnp.concatenate([q, k, f, jnp.zeros((hd - 3 * rows, hd), F32)], axis=0)
        cols = stack.T
        o_rows = jnp.zeros((rows, hd), F32)
        for j in range(rows):
            q_col = cols[:, j:j + 1]
            k_col = cols[:, rows + j:rows + j + 1]
            f_col = cols[:, 2 * rows + j:2 * rows + j + 1]
            s_new = s_ref[j, h] * f_col + k_col * v[j:j + 1, :]
            s_out_ref[j, h] = s_new
            o_j = jnp.sum(s_new * q_col, axis=0, keepdims=True)
            o_rows = jnp.where(rid == j, o_j, o_rows)
        o = o_rows * lax.rsqrt(jnp.mean(o_rows * o_rows, axis=-1, keepdims=True) + RMS_EPS)
        o_ref[rs, sl] = o * nw_ref[...] * _silu(zg)


def _hgrn_kernel(hq_ref, hf_ref, hi_ref, hg_ref, lb_ref, nw_ref, s_in_ref, o_ref, s_p_ref, s_s_ref, st_ref,
                 *, heads, batch, n_t, n_dec):
    b = pl.program_id(0)
    t = pl.program_id(1)

    @pl.when(jnp.logical_and(b < batch, t < n_t))
    def _():
        _hgrn_prompt_tile(hq_ref, hf_ref, hi_ref, hg_ref, lb_ref, nw_ref, o_ref, s_p_ref, st_ref,
                          t, n_t - 1, heads)

    @pl.when(jnp.logical_and(b == batch, t < n_dec))
    def _():
        _hgrn_decode_rows(hq_ref, hf_ref, hi_ref, hg_ref, lb_ref, nw_ref, s_in_ref, o_ref, s_s_ref, t, heads)


def _hgrn(h_all, lb_rows, norm_w, state, layer, batch, seq):
    n_s, heads = state.shape[1], state.shape[2]
    assert n_s == MIX_TILE, "the sample group must fill exactly one mixer tile"
    width = heads * HG_HEAD_DIM
    n_t = seq // MIX_TILE
    n_dec = n_s // DEC_ROWS
    m = batch * seq + n_s

    def row_blk(b, t):
        return jnp.where(b < batch, b * n_t + jnp.minimum(t, n_t - 1), batch * n_t)

    def dec_blk(b, t):
        return jnp.where(b == batch, jnp.minimum(t, n_dec - 1), 0)

    def hspec(j):
        return pl.BlockSpec((MIX_TILE, width), lambda b, t: (row_blk(b, t), j))

    s_shape = (DEC_ROWS, heads, HG_HEAD_DIM, HG_HEAD_DIM)
    s_in_spec = pl.BlockSpec((None,) + s_shape, lambda b, t: (layer, dec_blk(b, t), 0, 0, 0))
    s_spec = pl.BlockSpec(s_shape, lambda b, t: (dec_blk(b, t), 0, 0, 0))
    return pl.pallas_call(
        functools.partial(_hgrn_kernel, heads=heads, batch=batch, n_t=n_t, n_dec=n_dec),
        grid=(batch + 1, max(n_t, n_dec)),
        in_specs=[hspec(0), hspec(1), hspec(2), hspec(3),
                  pl.BlockSpec((3, width), lambda b, t: (0, 0)),
                  pl.BlockSpec((1, HG_HEAD_DIM), lambda b, t: (0, 0)),
                  s_in_spec],
        out_specs=[pl.BlockSpec((MIX_TILE, width), lambda b, t: (row_blk(b, t), 0)),
                   pl.BlockSpec((1, heads, HG_HEAD_DIM, HG_HEAD_DIM),
                                lambda b, t: (jnp.minimum(b, batch - 1), 0, 0, 0)),
                   s_spec],
        out_shape=[jax.ShapeDtypeStruct((m, width), F32),
                   jax.ShapeDtypeStruct((batch, heads, HG_HEAD_DIM, HG_HEAD_DIM), F32),
                   jax.ShapeDtypeStruct(state.shape[1:], F32)],
        scratch_shapes=[pltpu.VMEM((heads, HG_HEAD_DIM, HG_HEAD_DIM), F32)],
        compiler_params=pltpu.CompilerParams(dimension_semantics=("arbitrary", "arbitrary")),
        name="hgrn",
    )(h_all, h_all, h_all, h_all, lb_rows, norm_w, state)


def _rglru_terms(xc, gate_z, wa_ref, ba_ref, wx_ref, bx_ref, lam_ref):
    r = _sigmoid(_dot(xc, wa_ref[...]) + ba_ref[...])
    i = _sigmoid(_dot(xc, wx_ref[...]) + bx_ref[...])
    lam = lam_ref[...]
    log_sig_lam = jnp.minimum(lam, 0.0) - jnp.log1p(jnp.exp(-jnp.abs(lam)))
    log_a = RG_C * r * log_sig_lam
    a = jnp.exp(log_a)
    one_m_a2 = -jnp.tanh(log_a) * (a * a + 1.0)
    bterm = jnp.sqrt(one_m_a2) * (i * xc)
    return a, bterm, _gelu_tanh(gate_z)


def _rg_pool_prompt_tile(rx_ref, rgate_ref, pin_ref, w, o_rg_ref, o_pl_ref, h_out_ref, ctail_ref, ptail_ref,
                         rx_ext, pin_ext, h_carry, t_idx):
    cw_ref, cb_ref, wa_ref, ba_ref, wx_ref, bx_ref, lam_ref, pw_ref, ps_ref = w
    tile = MIX_TILE
    width = rx_ref.shape[1]
    conv_pad = SUBLANES
    pool_pad = 2 * SUBLANES

    @pl.when(t_idx == 0)
    def _():
        rx_ext[0:conv_pad, :] = jnp.zeros((conv_pad, width), F32)
        pin_ext[0:pool_pad, :] = jnp.zeros((pool_pad, width), F32)
        h_carry[...] = jnp.zeros_like(h_carry)

    x = rx_ref[...]
    rx_ext[conv_pad:conv_pad + tile, :] = x
    n_taps = cw_ref.shape[0]
    xc = x * cw_ref[n_taps - 1:n_taps, :] + cb_ref[...]
    for j in range(1, n_taps):
        xc = xc + rx_ext[conv_pad - j:conv_pad - j + tile, :] * cw_ref[n_taps - 1 - j:n_taps - j, :]
    tail = rx_ext[tile:tile + conv_pad, :]
    rx_ext[0:conv_pad, :] = tail
    ctail_ref[0] = tail

    a, bterm, gate = _rglru_terms(xc, rgate_ref[...], wa_ref, ba_ref, wx_ref, bx_ref, lam_ref)

    rid = lax.broadcasted_iota(I32, (tile, LANES), 0)
    for blk in range(width // LANES):
        sl = slice(blk * LANES, (blk + 1) * LANES)
        a_c, b_c = a[:, sl], bterm[:, sl]
        s = 1
        while s < tile:
            a_sh = jnp.where(rid >= s, pltpu.roll(a_c, s, axis=0), 1.0)
            b_sh = jnp.where(rid >= s, pltpu.roll(b_c, s, axis=0), 0.0)
            b_c = a_c * b_sh + b_c
            a_c = a_c * a_sh
            s *= 2
        hh = a_c * h_carry[:, sl] + b_c
        h_carry[:, sl] = hh[tile - 1:tile, :]
        o_rg_ref[:, sl] = hh * gate[:, sl]

    h_out_ref[0] = h_carry[...]

    p = pin_ref[...]
    pin_ext[pool_pad:pool_pad + tile, :] = p
    pos1 = t_idx * tile + rid + 1
    group = width // len(POOL_WINDOWS)
    pooled = []
    for g, win_len in enumerate(POOL_WINDOWS):
        sl = slice(g * group, (g + 1) * group)
        win = p[:, sl]
        for j in range(1, win_len):
            win = win + pin_ext[pool_pad - j:pool_pad - j + tile, sl]
        count = jnp.minimum(pos1, win_len).astype(F32)
        pooled.append(win / count - p[:, sl])
    ptail = pin_ext[tile:tile + pool_pad, :]
    pin_ext[0:pool_pad, :] = ptail
    ptail_ref[0] = ptail
    o_pl_ref[...] = _dot(jnp.concatenate(pooled, axis=1), pw_ref[...]) * ps_ref[...]


def _rg_pool_decode_rows(rx_ref, rgate_ref, pin_ref, cbuf_ref, h0_ref, pbuf_ref, w, o_rg_ref, o_pl_ref, h_out_ref):
    cw_ref, cb_ref, wa_ref, ba_ref, wx_ref, bx_ref, lam_ref, pw_ref, ps_ref = w
    x = rx_ref[...]
    n_taps = cw_ref.shape[0]
    xc = x * cw_ref[n_taps - 1:n_taps, :] + cb_ref[...]
    for j in range(n_taps - 1):
        xc = xc + cbuf_ref[j] * cw_ref[j:j + 1, :]
    a, bterm, gate = _rglru_terms(xc, rgate_ref[...], wa_ref, ba_ref, wx_ref, bx_ref, lam_ref)
    hh = a * h0_ref[...] + bterm
    h_out_ref[...] = hh
    o_rg_ref[...] = hh * gate

    p = pin_ref[...]
    n_buf = pbuf_ref.shape[0]
    width = p.shape[1]
    group = width // len(POOL_WINDOWS)
    pooled = []
    for g, win_len in enumerate(POOL_WINDOWS):
        sl = slice(g * group, (g + 1) * group)
        win = p[:, sl]
        for j in range(1, win_len):
            win = win + pbuf_ref[n_buf - j, :, sl]
        pooled.append(win / float(min(PAST_LEN + 1, win_len)) - p[:, sl])
    o_pl_ref[...] = _dot(jnp.concatenate(pooled, axis=1), pw_ref[...]) * ps_ref[...]


def _rg_pool_kernel(rx_ref, rgate_ref, pin_ref, cbuf_ref, h0_ref, pbuf_ref,
                    cw_ref, cb_ref, wa_ref, ba_ref, wx_ref, bx_ref, lam_ref, pw_ref, ps_ref,
                    o_rg_ref, o_pl_ref, h_p_ref, ctail_ref, ptail_ref, h_s_ref,
                    rx_ext, pin_ext, h_carry, *, batch):
    b = pl.program_id(0)
    t = pl.program_id(1)
    w = (cw_ref, cb_ref, wa_ref, ba_ref, wx_ref, bx_ref, lam_ref, pw_ref, ps_ref)

    @pl.when(b < batch)
    def _():
        _rg_pool_prompt_tile(rx_ref, rgate_ref, pin_ref, w, o_rg_ref, o_pl_ref, h_p_ref, ctail_ref, ptail_ref,
                             rx_ext, pin_ext, h_carry, t)

    @pl.when(jnp.logical_and(b == batch, t == 0))
    def _():
        _rg_pool_decode_rows(rx_ref, rgate_ref, pin_ref, cbuf_ref, h0_ref, pbuf_ref, w, o_rg_ref, o_pl_ref, h_s_ref)


def _rg_pool(h_all, cbuf_t, h0, pbuf_t, rg_w, pool_w, batch, seq, col0):
    n_s, width = h0.shape
    assert n_s == MIX_TILE, "the sample group must fill exactly one mixer tile"
    n_t = seq // MIX_TILE
    c0 = col0 // width
    m = batch * seq + n_s

    def row_blk(b, t):
        return jnp.where(b < batch, b * n_t + t, batch * n_t)

    def prompt_blk(b):
        return jnp.minimum(b, batch - 1)

    def hspec(j):
        return pl.BlockSpec((MIX_TILE, width), lambda b, t: (row_blk(b, t), c0 + j))

    def full(arr):
        return pl.BlockSpec(arr.shape, lambda b, t: (0,) * arr.ndim)

    weights = [rg_w["conv_w"], rg_w["conv_b"], rg_w["wa"], rg_w["ba"], rg_w["wx"], rg_w["bx"], rg_w["lam"],
               pool_w["w"], pool_w["scale"]]
    o_spec = pl.BlockSpec((MIX_TILE, width), lambda b, t: (row_blk(b, t), 0))
    return pl.pallas_call(
        functools.partial(_rg_pool_kernel, batch=batch),
        grid=(batch + 1, n_t),
        in_specs=[hspec(0), hspec(1), hspec(2), full(cbuf_t), full(h0), full(pbuf_t)] + [full(w) for w in weights],
        out_specs=[o_spec, o_spec,
                   pl.BlockSpec((1, 1, width), lambda b, t: (prompt_blk(b), 0, 0)),
                   pl.BlockSpec((1, SUBLANES, width), lambda b, t: (prompt_blk(b), 0, 0)),
                   pl.BlockSpec((1, 2 * SUBLANES, width), lambda b, t: (prompt_blk(b), 0, 0)),
                   pl.BlockSpec((n_s, width), lambda b, t: (0, 0))],
        out_shape=[jax.ShapeDtypeStruct((m, width), F32), jax.ShapeDtypeStruct((m, width), F32),
                   jax.ShapeDtypeStruct((batch, 1, width), F32),
                   jax.ShapeDtypeStruct((batch, SUBLANES, width), F32),
                   jax.ShapeDtypeStruct((batch, 2 * SUBLANES, width), F32),
                   jax.ShapeDtypeStruct((n_s, width), F32)],
        scratch_shapes=[pltpu.VMEM((MIX_TILE + SUBLANES, width), F32),
                        pltpu.VMEM((MIX_TILE + 2 * SUBLANES, width), F32),
                        pltpu.VMEM((1, width), F32)],
        compiler_params=pltpu.CompilerParams(dimension_semantics=("arbitrary", "arbitrary")),
        name="rg_pool",
    )(h_all, h_all, h_all, cbuf_t, h0, pbuf_t, *weights)


def _post_mix_kernel(x_ref, o_hg_ref, o_rg_ref, o_pl_ref, w_ref, g_ref, b_ref, rwh_ref, rwl_ref, rb_ref,
                     x1_ref, idx_ref, gate_ref, *, alpha, n_experts):
    n_hg = o_hg_ref.shape[1]
    n_rg = o_rg_ref.shape[1]
    y = _dot(o_hg_ref[...], w_ref[0:n_hg, :])
    y = y + _dot(o_rg_ref[...], w_ref[n_hg:n_hg + n_rg, :])
    y = y + _dot(o_pl_ref[...], w_ref[n_hg + n_rg:, :])
    x1 = _layer_norm(alpha * x_ref[...] + y, g_ref[...], b_ref[...])
    x1_ref[...] = x1

    x_hi = x1.astype(BF16)
    x_lo = (x1 - x_hi.astype(F32)).astype(BF16)
    logits = (jnp.dot(x_hi, rwh_ref[...], preferred_element_type=F32)
              + jnp.dot(x_lo, rwh_ref[...], preferred_element_type=F32)
              + jnp.dot(x_hi, rwl_ref[...], preferred_element_type=F32)) + rb_ref[...]

    lane = lax.broadcasted_iota(I32, logits.shape, 1).astype(F32)
    vals = jnp.where(lane < n_experts, logits, NEG_BIG)
    idx_out = jnp.zeros(logits.shape, F32)
    exp_out = jnp.zeros(logits.shape, F32)
    v0 = None
    denom = None
    for k in range(TOP_K):
        m = jnp.max(vals, axis=-1, keepdims=True)
        idx = jnp.min(jnp.where(vals == m, lane, float(LANES)), axis=-1, keepdims=True)
        if k == 0:
            v0 = m
        e = jnp.exp(m - v0)
        denom = e if denom is None else denom + e
        idx_out = jnp.where(lane == k, idx, idx_out)
        exp_out = jnp.where(lane == k, e, exp_out)
        vals = jnp.where(lane == idx, 2.0 * NEG_BIG, vals)
    idx_ref[...] = idx_out.astype(I32)
    gate_ref[...] = exp_out / denom


def _post_mix(x, o_hg, o_rg, o_pl, w_out_bf16, layer, ln_g, ln_b, rw_hi, rw_lo, rb_pad, alpha, n_experts):
    m, d = x.shape

    def rows(arr):
        return pl.BlockSpec((POST_TM, arr.shape[1]), lambda i: (i, 0))

    def full(arr):
        return pl.BlockSpec(arr.shape, lambda i: (0,) * arr.ndim)

    w_spec = pl.BlockSpec((None,) + w_out_bf16.shape[1:], lambda i: (layer, 0, 0))
    return pl.pallas_call(
        functools.partial(_post_mix_kernel, alpha=alpha, n_experts=n_experts),
        grid=(m // POST_TM,),
        in_specs=[rows(x), rows(o_hg), rows(o_rg), rows(o_pl), w_spec, full(ln_g), full(ln_b),
                  full(rw_hi), full(rw_lo), full(rb_pad)],
        out_specs=[pl.BlockSpec((POST_TM, d), lambda i: (i, 0)),
                   pl.BlockSpec((POST_TM, LANES), lambda i: (i, 0)),
                   pl.BlockSpec((POST_TM, LANES), lambda i: (i, 0))],
        out_shape=[jax.ShapeDtypeStruct((m, d), F32), jax.ShapeDtypeStruct((m, LANES), I32),
                   jax.ShapeDtypeStruct((m, LANES), F32)],
        compiler_params=pltpu.CompilerParams(dimension_semantics=("parallel",),
                                             vmem_limit_bytes=VMEM_BYTES_V7X * 3 // 4),
        name="post_mix",
    )(x, o_hg, o_rg, o_pl, w_out_bf16, ln_g, ln_b, rw_hi, rw_lo, rb_pad)


def _for_rows(lo, hi, fn):
    g_lo = lo >> SUBLANE_SHIFT
    g_hi = hi >> SUBLANE_SHIFT

    def group_body(g, carry):
        for u in range(SUBLANES):
            fn(g, u)
        return carry

    def single_body(r, carry):
        fn(g_hi, r - (g_hi << SUBLANE_SHIFT))
        return carry

    lax.fori_loop(g_lo, g_hi, group_body, 0)
    lax.fori_loop(g_hi << SUBLANE_SHIFT, hi, single_body, 0)


def _moe_kernel(item_expert, item_rows, item_start, order_hbm, x_hbm, wgu_ref, bgu_ref, wd_ref, bd_ref, y_hbm,
                acc_ref, stage_ref, xg_ref, idx_ref, ctr_ref, sem_idx, sem_in, sem_out, *, n_tokens, n_chunks):
    del item_expert
    w = pl.program_id(0)
    c = pl.program_id(1)
    n_items = pl.num_programs(0)
    last_c = n_chunks - 1
    n = item_rows[w]
    n_blk = (n + MOE_BLK - 1) // MOE_BLK
    d = xg_ref.shape[1]
    tn = wgu_ref.shape[1]
    blk_groups = MOE_BLK // SUBLANES
    nxt = jnp.minimum(w + 1, n_items - 1)
    n_next = jnp.where(w + 1 < n_items, item_rows[nxt], 0)
    this_win = pl.multiple_of((w & 1) * MOE_IDX_LEN, SMEM_1D_TILE)
    next_win = pl.multiple_of((1 - (w & 1)) * MOE_IDX_LEN, SMEM_1D_TILE)

    def window_offset(item):
        return item_start[item] & (SMEM_1D_TILE - 1)

    def window_copy(item, win):
        src0 = pl.multiple_of(item_start[item] - window_offset(item), SMEM_1D_TILE)
        return pltpu.make_async_copy(order_hbm.at[pl.ds(src0, MOE_IDX_LEN)], idx_ref.at[pl.ds(win, MOE_IDX_LEN)],
                                     sem_idx)

    def gather_row(first, last_row, g, u):
        row = jnp.minimum((g << SUBLANE_SHIFT) + u, last_row)
        pair = idx_ref[first + row]
        pltpu.make_async_copy(x_hbm.at[pl.ds(pair >> TOP_K_SHIFT, 1)], stage_ref.at[g, pl.ds(u, 1)],
                              sem_in).start()

    def start_gather(item, win, lo, hi):
        first = win + window_offset(item)
        last_row = jnp.maximum(item_rows[item] - 1, 0)
        _for_rows(lo, hi, functools.partial(gather_row, first, last_row))

    def start_scatter(lo, hi):
        first = this_win + window_offset(w)

        def issue(g, u):
            pair = idx_ref[first + (g << SUBLANE_SHIFT) + u]
            dst = (pair & (TOP_K - 1)) * n_tokens + (pair >> TOP_K_SHIFT)
            pltpu.make_async_copy(acc_ref.at[g, pl.ds(u, 1)], y_hbm.at[pl.ds(dst, 1)], sem_out).start()
        _for_rows(lo, hi, issue)

    def wait_gather(count):
        _for_rows(0, count, lambda g, u: pltpu.make_async_copy(
            x_hbm.at[pl.ds(0, 1)], stage_ref.at[0, pl.ds(0, 1)], sem_in).wait())

    def wait_scatter(count):
        _for_rows(0, count, lambda g, u: pltpu.make_async_copy(
            acc_ref.at[0, pl.ds(0, 1)], y_hbm.at[pl.ds(0, 1)], sem_out).wait())

    @pl.when(jnp.logical_and(w == 0, c == 0))
    def _():
        stage_ref[...] = jnp.zeros_like(stage_ref)

        @pl.when(n > 0)
        def _():
            window_copy(0, this_win).start()
            window_copy(0, this_win).wait()
            start_gather(0, this_win, 0, n)
            ctr_ref[1] = n

    @pl.when(n > 0)
    def _():
        @pl.when(c == 0)
        def _():
            window_copy(nxt, next_win).start()
            wait_gather(ctr_ref[1])

            @pl.when(w > 0)
            def _():
                wait_scatter(item_rows[jnp.maximum(w - 1, 0)])

            def cast_body(i, carry):
                gs = pl.ds(pl.multiple_of(i * blk_groups, blk_groups), blk_groups)
                rs = pl.ds(pl.multiple_of(i * MOE_BLK, MOE_BLK), MOE_BLK)
                xg_ref[rs, :] = stage_ref[gs].reshape(MOE_BLK, d).astype(BF16)
                acc_ref[gs] = jnp.broadcast_to(bd_ref[...], (blk_groups, SUBLANES, d))
                return carry
            lax.fori_loop(0, n_blk, cast_body, 0)
            window_copy(nxt, next_win).wait()
            ctr_ref[0] = 0

        next_first = next_win + window_offset(nxt)
        next_last_row = jnp.maximum(n_next - 1, 0)

        sel_r = lax.broadcasted_iota(I32, (tn, tn // 2), 0)
        sel_c = lax.broadcasted_iota(I32, (tn, tn // 2), 1)
        pick = jnp.where(sel_r == 2 * sel_c, 1.0, 0.0).astype(BF16)

        def sub_tile(row0, rows):
            groups = rows // SUBLANES
            gs = pl.ds(pl.multiple_of(row0 >> SUBLANE_SHIFT, blk_groups), groups)
            h = jnp.dot(xg_ref[pl.ds(row0, rows), :], wgu_ref[...].astype(BF16),
                        preferred_element_type=F32) + bgu_ref[...]
            h_next = pltpu.roll(h, tn - 1, axis=1)
            gate = jnp.minimum(h, SWIGLU_LIMIT)
            up = jnp.clip(h_next, -SWIGLU_LIMIT, SWIGLU_LIMIT)
            act = (up + 1.0) * gate * _sigmoid(gate * SWIGLU_ALPHA)
            act_dense = jnp.dot(act.astype(BF16), pick, preferred_element_type=F32)
            upd = jnp.dot(act_dense.astype(BF16), wd_ref[...].astype(BF16), preferred_element_type=F32)
            acc_ref[gs] = acc_ref[gs] + upd.reshape(groups, SUBLANES, d)

            issued = ctr_ref[0]
            g_base = issued >> SUBLANE_SHIFT
            for k in range(rows // n_chunks):
                gather_row(next_first, next_last_row, g_base + k // SUBLANES, k % SUBLANES)
            ctr_ref[0] = issued + rows // n_chunks

            @pl.when(c == last_c)
            def _():
                start_scatter(row0, jnp.minimum(row0 + rows, n))

        big = MOE_SUB_ROWS[0]
        n_big = n_blk // (big // MOE_BLK)

        def big_body(i, carry):
            sub_tile(pl.multiple_of(i * big, big), big)
            return carry
        lax.fori_loop(0, n_big, big_body, 0)
        row = n_big * big
        for rows in MOE_SUB_ROWS[1:]:
            take = (n_blk & (rows // MOE_BLK)) != 0

            @pl.when(take)
            def _(row=row, rows=rows):
                sub_tile(pl.multiple_of(row, MOE_BLK), rows)
            row = row + jnp.where(take, rows, 0)

        @pl.when(c == last_c)
        def _():
            issued = ctr_ref[0]

            @pl.when(issued < n_next)
            def _():
                start_gather(nxt, next_win, issued, n_next)
            total = jnp.maximum(issued, n_next)
            ctr_ref[1] = total

            @pl.when(n_next == 0)
            def _():
                wait_scatter(n)
                wait_gather(total)


def _moe(x1, tables, wgu, bgu, wd, bd, layer, n_items):
    m, d = x1.shape
    depth, n_exp, _, n_gu = wgu.shape
    n_chunks = n_gu // MOE_TN
    half = MOE_TN // 2
    row_groups = MOE_TM // SUBLANES

    def chunk(c, rows_ref, w):
        return jnp.where(rows_ref[w] > 0, c, n_chunks - 1)

    grid_spec = pltpu.PrefetchScalarGridSpec(
        num_scalar_prefetch=3,
        grid=(n_items, n_chunks),
        in_specs=[
            pl.BlockSpec(memory_space=pl.ANY),
            pl.BlockSpec(memory_space=pl.ANY),
            pl.BlockSpec((None, None, d, MOE_TN), lambda w, c, ie, ir, ist: (layer, ie[w], 0, chunk(c, ir, w))),
            pl.BlockSpec((None, None, 1, MOE_TN), lambda w, c, ie, ir, ist: (layer, ie[w], 0, chunk(c, ir, w))),
            pl.BlockSpec((None, None, half, d), lambda w, c, ie, ir, ist: (layer, ie[w], chunk(c, ir, w), 0)),
            pl.BlockSpec((None, None, 1, d), lambda w, c, ie, ir, ist: (layer, ie[w], 0, 0)),
        ],
        out_specs=pl.BlockSpec(memory_space=pl.ANY),
        scratch_shapes=[pltpu.VMEM((row_groups, SUBLANES, d), F32), pltpu.VMEM((row_groups, SUBLANES, d), F32),
                        pltpu.VMEM((MOE_TM, d), BF16),
                        pltpu.SMEM((2 * MOE_IDX_LEN,), I32), pltpu.SMEM((2,), I32),
                        pltpu.SemaphoreType.DMA, pltpu.SemaphoreType.DMA, pltpu.SemaphoreType.DMA],
    )
    assert all((rows // n_chunks) % SUBLANES == 0 for rows in MOE_SUB_ROWS)
    return pl.pallas_call(
        functools.partial(_moe_kernel, n_tokens=m, n_chunks=n_chunks),
        grid_spec=grid_spec,
        out_shape=jax.ShapeDtypeStruct((TOP_K * m, d), F32),
        compiler_params=pltpu.CompilerParams(dimension_semantics=("arbitrary", "arbitrary"),
                                             vmem_limit_bytes=VMEM_BYTES_V7X * 7 // 8),
        name="moe_experts",
    )(tables["expert"], tables["rows"], tables["start"], tables["order"], x1,
      wgu, bgu.reshape(depth, n_exp, 1, n_gu), wd, bd.reshape(depth, n_exp, 1, d))


def _route_tables(top_idx, n_experts, n_items):
    m = top_idx.shape[0]
    n_pairs = m * TOP_K
    pair_expert = top_idx.reshape(-1)
    order = jnp.argsort(pair_expert, stable=True).astype(I32)
    counts = jnp.sum((pair_expert[:, None] == jnp.arange(n_experts, dtype=I32)[None, :]).astype(I32), axis=0)
    row_end = jnp.cumsum(counts)
    row_off = row_end - counts
    tiles = (counts + MOE_TM - 1) // MOE_TM
    tile_end = jnp.cumsum(tiles)
    tile_off = tile_end - tiles
    total = tile_end[-1]
    w = jnp.arange(n_items, dtype=I32)
    e_w = jnp.minimum(jnp.searchsorted(tile_end, w, side="right").astype(I32), n_experts - 1)
    j = w - tile_off[e_w]
    active = w < total
    start = jnp.where(active, row_off[e_w] + j * MOE_TM, 0)
    rows = jnp.where(active, jnp.clip(counts[e_w] - j * MOE_TM, 0, MOE_TM), 0)
    e_w = jnp.where(active, e_w, e_w[jnp.maximum(total - 1, 0)])
    padded = -(-n_pairs // SMEM_1D_TILE) * SMEM_1D_TILE + MOE_IDX_LEN
    return {"expert": e_w.astype(I32), "rows": rows.astype(I32), "start": start.astype(I32),
            "order": jnp.pad(order, (0, padded - n_pairs))}


def _combine_kernel(x1_ref, gate_ref, y_ref, g_ref, b_ref, o_ref, *, alpha):
    gates = gate_ref[...]
    acc = alpha * x1_ref[...]
    for k in range(TOP_K):
        acc = acc + gates[:, k:k + 1] * y_ref[k]
    o_ref[...] = _layer_norm(acc, g_ref[...], b_ref[...])


def _combine(x1, gates, y, ln_g, ln_b, alpha):
    m, d = x1.shape
    return pl.pallas_call(
        functools.partial(_combine_kernel, alpha=alpha),
        grid=(m // COMBINE_TM,),
        in_specs=[pl.BlockSpec((COMBINE_TM, d), lambda i: (i, 0)),
                  pl.BlockSpec((COMBINE_TM, LANES), lambda i: (i, 0)),
                  pl.BlockSpec((TOP_K, COMBINE_TM, d), lambda i: (0, i, 0)),
                  pl.BlockSpec((1, d), lambda i: (0, 0)),
                  pl.BlockSpec((1, d), lambda i: (0, 0))],
        out_specs=pl.BlockSpec((COMBINE_TM, d), lambda i: (i, 0)),
        out_shape=jax.ShapeDtypeStruct((m, d), F32),
        compiler_params=pltpu.CompilerParams(dimension_semantics=("parallel",),
                                             vmem_limit_bytes=VMEM_BYTES_V7X * 3 // 4),
        name="moe_combine",
    )(x1, gates, y.reshape(TOP_K, m, d), ln_g, ln_b)


def _block_diag(w):
    nb, n, _ = w.shape
    eye = jnp.eye(nb, dtype=w.dtype)
    return (eye[:, None, :, None] * w[:, :, None, :]).reshape(nb * n, nb * n)


def _row(v):
    return v.reshape(1, -1).astype(F32)


def kernel(x_prompt, x_sample, state_hgrn, state_rglru_h, state_rglru_conv, state_pool, w_in, hg_lb_logits,
           hg_norm_w, rg_conv_w, rg_conv_b, rg_wa, rg_ba, rg_wx, rg_bx, rg_lambda, pool_w, pool_scale, w_out,
           ln1_g, ln1_b, router_w, router_b, moe_w_gate_up, moe_b_gate_up, moe_w_down, moe_b_down, ln2_g, ln2_b):
    depth = w_in.shape[0]
    bp, lp, d_model = x_prompt.shape
    bs = x_sample.shape[0]
    n_p = bp * lp
    m = n_p + bs
    hg_width = hg_lb_logits.shape[1]
    rg_width = rg_lambda.shape[1]
    n_experts = router_w.shape[2]
    n_conv = state_rglru_conv.shape[2]
    n_pool = state_pool.shape[2]
    alpha = float((2 * depth) ** 0.25)
    n_items = n_experts + (m * TOP_K) // MOE_TM
    rg_col0 = 4 * hg_width

    p_lb = jax.nn.softmax(hg_lb_logits.astype(F32), axis=0)
    lbs = jnp.cumsum(p_lb, axis=0)
    lbs = lbs - lbs[0]

    w_in_bf16 = w_in.astype(BF16)
    w_out_bf16 = w_out.astype(BF16)
    x = jnp.concatenate([x_prompt.reshape(n_p, d_model), x_sample.reshape(bs, d_model)], axis=0)
    outs = {k: [] for k in ("hg_p", "hg_s", "h_p", "h_s", "c_p", "c_s", "pl_p", "pl_s")}
    for l in range(depth):
        lb = lbs[l]
        lb_rows = jnp.stack([jnp.log(lb), jnp.log1p(-lb), 1.0 - lb], axis=0)
        rg_w = {"conv_w": rg_conv_w[l], "conv_b": _row(rg_conv_b[l]), "wa": _block_diag(rg_wa[l]).astype(BF16),
                "ba": _row(rg_ba[l]), "wx": _block_diag(rg_wx[l]).astype(BF16), "bx": _row(rg_bx[l]),
                "lam": _row(rg_lambda[l])}
        pl_w = {"w": _block_diag(pool_w[l]).astype(BF16), "scale": _row(pool_scale[l])}

        h_all = _in_proj(x, w_in_bf16, l)
        o_hg, s_p, s_s = _hgrn(h_all, lb_rows, _row(hg_norm_w[l]), state_hgrn, l, bp, lp)
        o_rg, o_pl, h_p, c_tail, p_tail, h_s = _rg_pool(
            h_all, jnp.swapaxes(state_rglru_conv[l], 0, 1), state_rglru_h[l], jnp.swapaxes(state_pool[l], 0, 1),
            rg_w, pl_w, bp, lp, rg_col0)

        rx_s = h_all[n_p:, rg_col0:rg_col0 + rg_width]
        pin_s = h_all[n_p:, rg_col0 + 2 * rg_width:]
        outs["hg_p"].append(s_p)
        outs["hg_s"].append(s_s)
        outs["h_p"].append(h_p.reshape(bp, rg_width))
        outs["h_s"].append(h_s)
        outs["c_p"].append(c_tail[:, SUBLANES - n_conv:])
        outs["c_s"].append(jnp.concatenate([state_rglru_conv[l][:, 1:], rx_s[:, None, :]], axis=1))
        outs["pl_p"].append(p_tail[:, 2 * SUBLANES - n_pool:])
        outs["pl_s"].append(jnp.concatenate([state_pool[l][:, 1:], pin_s[:, None, :]], axis=1))

        rw = jnp.pad(router_w[l].astype(F32), ((0, 0), (0, LANES - n_experts)))
        rw_hi = rw.astype(BF16)
        rw_lo = (rw - rw_hi.astype(F32)).astype(BF16)
        rb_pad = jnp.pad(router_b[l].astype(F32), (0, LANES - n_experts)).reshape(1, LANES)
        x1, top_idx, gates = _post_mix(x, o_hg, o_rg, o_pl, w_out_bf16, l, _row(ln1_g[l]), _row(ln1_b[l]),
                                       rw_hi, rw_lo, rb_pad, alpha, n_experts)

        tables = _route_tables(top_idx[:, :TOP_K], n_experts, n_items)
        y = _moe(x1, tables, moe_w_gate_up, moe_b_gate_up, moe_w_down, moe_b_down, l, n_items)
        x = _combine(x1, gates, y, _row(ln2_g[l]), _row(ln2_b[l]), alpha)

    return (x[:n_p].reshape(bp, lp, d_model), x[n_p:].reshape(bs, 1, d_model),
            jnp.stack(outs["hg_p"]), jnp.stack(outs["hg_s"]), jnp.stack(outs["h_p"]), jnp.stack(outs["h_s"]),
            jnp.stack(outs["c_p"]), jnp.stack(outs["c_s"]), jnp.stack(outs["pl_p"]), jnp.stack(outs["pl_s"]))
```

```python
import functools

import jax
import jax.numpy as jnp
from jax import lax
from jax.experimental import pallas as pl
from jax.experimental.pallas import tpu as pltpu

F32 = jnp.float32
BF16 = jnp.bfloat16
I32 = jnp.int32

HG_HEAD_DIM = 128
RG_C = 8.0
POOL_WINDOWS = (2, 4, 8, 16)
TOP_K = 4
TOP_K_SHIFT = TOP_K.bit_length() - 1
SWIGLU_LIMIT = 7.0
SWIGLU_ALPHA = 1.702
LN_EPS = 1e-5
RMS_EPS = 1e-6
PAST_LEN = 16384
GELU_C0 = 0.7978845608028654
GELU_C1 = 0.044715

LANES = 128
SUBLANES = 8
SUBLANE_SHIFT = SUBLANES.bit_length() - 1
VMEM_BYTES_V7X = 64 * 1024 * 1024
SMEM_1D_TILE = 1024

PROJ_TM = 640
PROJ_TN = 1408
POST_TM = 320
MIX_TILE = 128
DEC_ROWS = 8
MOE_TM = 1536
MOE_BLK = 128
MOE_SUB_ROWS = (512, 256, 128)
MOE_TN = 512
MOE_IDX_LEN = -(-(MOE_TM + SMEM_1D_TILE - 1) // SMEM_1D_TILE) * SMEM_1D_TILE
COMBINE_TM = 320
NEG_BIG = -1e30

assert TOP_K == 1 << TOP_K_SHIFT


def _sigmoid(x):
    return jax.nn.sigmoid(x)


def _silu(x):
    return x * _sigmoid(x)


def _gelu_tanh(x):
    return 0.5 * x * (1.0 + jnp.tanh(GELU_C0 * (x + GELU_C1 * (x * x * x))))


def _dot(a, b):
    return jnp.dot(a.astype(BF16), b.astype(BF16), preferred_element_type=F32)


def _dot_nt(a, b):
    return lax.dot_general(a.astype(BF16), b.astype(BF16), (((1,), (1,)), ((), ())),
                           preferred_element_type=F32)


def _dot_tn(a, b):
    return lax.dot_general(a.astype(BF16), b.astype(BF16), (((0,), (0,)), ((), ())),
                           preferred_element_type=F32)


def _layer_norm(x, g, b):
    mu = jnp.mean(x, axis=-1, keepdims=True)
    xc = x - mu
    var = jnp.mean(xc * xc, axis=-1, keepdims=True)
    return xc * lax.rsqrt(var + LN_EPS) * g + b


def _in_proj_kernel(x_ref, w_ref, o_ref):
    o_ref[...] = jnp.dot(x_ref[...].astype(BF16), w_ref[...], preferred_element_type=F32)


def _in_proj(x, w_bf16, layer):
    m, k = x.shape
    n = w_bf16.shape[2]
    return pl.pallas_call(
        _in_proj_kernel,
        grid=(m // PROJ_TM, n // PROJ_TN),
        in_specs=[pl.BlockSpec((PROJ_TM, k), lambda i, j: (i, 0)),
                  pl.BlockSpec((None, k, PROJ_TN), lambda i, j: (layer, 0, j))],
        out_specs=pl.BlockSpec((PROJ_TM, PROJ_TN), lambda i, j: (i, j)),
        out_shape=jax.ShapeDtypeStruct((m, n), F32),
        compiler_params=pltpu.CompilerParams(dimension_semantics=("parallel", "arbitrary"),
                                             vmem_limit_bytes=VMEM_BYTES_V7X * 3 // 4),
        name="in_proj",
    )(x, w_bf16)


def _hgrn_gates(z, log_lb, log1m_lb, one_m_lb):
    e = jnp.exp(-jnp.abs(z))
    log_sig = jnp.minimum(z, 0.0) - jnp.log1p(e)
    c = log1m_lb + log_sig
    logf = jnp.maximum(log_lb, c) + jnp.log1p(jnp.exp(-jnp.abs(log_lb - c)))
    k = one_m_lb * (jnp.where(z > 0, e, 1.0) / (1.0 + e))
    return logf, k


def _cumsum_rows(x, rid):
    n = x.shape[0]
    s = 1
    while s < n:
        x = x + jnp.where(rid >= s, pltpu.roll(x, s, axis=0), 0.0)
        s *= 2
    return x


def _hgrn_prompt_tile(hq_ref, hf_ref, hi_ref, hg_ref, lb_ref, nw_ref, o_ref, s_out_ref, st_ref,
                      t_idx, last_t, heads):
    tile, hd = MIX_TILE, HG_HEAD_DIM
    half, quarter, mid = tile // 2, tile // 4, tile // 8

    @pl.when(t_idx == 0)
    def _():
        st_ref[...] = jnp.zeros_like(st_ref)

    row = lax.broadcasted_iota(I32, (tile, tile), 0)
    col = lax.broadcasted_iota(I32, (tile, tile), 1)
    rid = lax.broadcasted_iota(I32, (tile, hd), 0)
    mask_a = (row >= half) & (col < half)
    q_shift = quarter.bit_length() - 1
    row_q, col_q = row >> q_shift, col >> q_shift
    mask_b = (row_q == col_q + 1) & ((col_q & 1) == 0)
    mask_d = (row_q == col_q) & (col <= row)

    def head_body(h):
        sl = slice(h * hd, (h + 1) * hd)
        zq = hq_ref[:, sl]
        zf = hf_ref[:, sl]
        v = hi_ref[:, sl]
        zg = hg_ref[:, sl]
        q = _silu(zq)
        logf, k = _hgrn_gates(zf, lb_ref[0:1, sl], lb_ref[1:2, sl], lb_ref[2:3, sl])
        b = _cumsum_rows(logf, rid)

        def split_level(ref_rows):
            d = b - ref_rows
            e = jnp.exp(-jnp.abs(d))
            return q * jnp.where(d < 0, e, 1.0), k * jnp.where(d < 0, 1.0, e)

        qa, ka = split_level(b[half - 1:half, :])
        ref_b = jnp.where(rid < half, b[quarter - 1:quarter, :], b[half + quarter - 1:half + quarter, :])
        qb, kb = split_level(ref_b)
        ref_d = b[mid - 1:mid, :]
        for blk in range(1, 4):
            lo = blk * quarter
            ref_d = jnp.where(rid >= lo, b[lo + mid - 1:lo + mid, :], ref_d)
        dd = b - ref_d
        qd = q * jnp.exp(dd)
        kd = k * jnp.exp(-dd)

        scores = jnp.where(mask_a, _dot_nt(qa, ka),
                           jnp.where(mask_b, _dot_nt(qb, kb),
                                     jnp.where(mask_d, _dot_nt(qd, kd), 0.0)))
        st = st_ref[h]
        o = _dot(scores, v) + _dot_nt(q * jnp.exp(b), st)
        b_last = b[tile - 1:tile, :]
        st_new = st * jnp.exp(b_last) + _dot_tn(v, k * jnp.exp(b_last - b))
        st_ref[h] = st_new

        o = o * lax.rsqrt(jnp.mean(o * o, axis=-1, keepdims=True) + RMS_EPS) * nw_ref[...] * _silu(zg)
        o_ref[:, sl] = o

    for h in range(heads):
        head_body(h)

    @pl.when(t_idx == last_t)
    def _():
        for h in range(heads):
            s_out_ref[0, h] = st_ref[h].T


def _hgrn_decode_rows(hq_ref, hf_ref, hi_ref, hg_ref, lb_ref, nw_ref, s_ref, o_ref, s_out_ref, step, heads):
    hd = HG_HEAD_DIM
    rows = DEC_ROWS
    rid = lax.broadcasted_iota(I32, (rows, hd), 0)
    rs = pl.ds(pl.multiple_of(step * rows, rows), rows)

    for h in range(heads):
        sl = slice(h * hd, (h + 1) * hd)
        zq = hq_ref[rs, sl]
        zf = hf_ref[rs, sl]
        v = hi_ref[rs, sl]
        zg = hg_ref[rs, sl]
        q = _silu(zq)
        logf, k = _hgrn_gates(zf, lb_ref[0:1, sl], lb_ref[1:2, sl], lb_ref[2:3, sl])
        f = jnp.exp(logf)
        stack = jnp.concatenate([q, k, f, jnp.zeros((hd - 3 * rows, hd), F32)], axis=0)
        cols = stack.T
        o_rows = jnp.zeros((rows, hd), F32)
        for j in range(rows):
            q_col = cols[:, j:j + 1]
            k_col = cols[:, rows + j:rows + j + 1]
            f_col = cols[:, 2 * rows + j:2 * rows + j + 1]
            s_new = s_ref[j, h] * f_col + k_col * v[j:j + 1, :]
            s_out_ref[j, h] = s_new
            o_j = jnp.sum(s_new * q_col, axis=0, keepdims=True)
            o_rows = jnp.where(rid == j, o_j, o_rows)
        o = o_rows * lax.rsqrt(jnp.mean(o_rows * o_rows, axis=-1, keepdims=True) + RMS_EPS)
        o_ref[rs, sl] = o * nw_ref[...] * _silu(zg)


def _hgrn_kernel(hq_ref, hf_ref, hi_ref, hg_ref, lb_ref, nw_ref, s_in_ref, o_ref, s_p_ref, s_s_ref, st_ref,
                 *, heads, batch, n_t, n_dec):
    b = pl.program_id(0)
    t = pl.program_id(1)

    @pl.when(jnp.logical_and(b < batch, t < n_t))
    def _():
        _hgrn_prompt_tile(hq_ref, hf_ref, hi_ref, hg_ref, lb_ref, nw_ref, o_ref, s_p_ref, st_ref,
                          t, n_t - 1, heads)

    @pl.when(jnp.logical_and(b == batch, t < n_dec))
    def _():
        _hgrn_decode_rows(hq_ref, hf_ref, hi_ref, hg_ref, lb_ref, nw_ref, s_in_ref, o_ref, s_s_ref, t, heads)


def _hgrn(h_all, lb_rows, norm_w, state, layer, batch, seq):
    n_s, heads = state.shape[1], state.shape[2]
    assert n_s == MIX_TILE, "the sample group must fill exactly one mixer tile"
    width = heads * HG_HEAD_DIM
    n_t = seq // MIX_TILE
    n_dec = n_s // DEC_ROWS
    m = batch * seq + n_s

    def row_blk(b, t):
        return jnp.where(b < batch, b * n_t + jnp.minimum(t, n_t - 1), batch * n_t)

    def dec_blk(b, t):
        return jnp.where(b == batch, jnp.minimum(t, n_dec - 1), 0)

    def hspec(j):
        return pl.BlockSpec((MIX_TILE, width), lambda b, t: (row_blk(b, t), j))

    s_shape = (DEC_ROWS, heads, HG_HEAD_DIM, HG_HEAD_DIM)
    s_in_spec = pl.BlockSpec((None,) + s_shape, lambda b, t: (layer, dec_blk(b, t), 0, 0, 0))
    s_spec = pl.BlockSpec(s_shape, lambda b, t: (dec_blk(b, t), 0, 0, 0))
    return pl.pallas_call(
        functools.partial(_hgrn_kernel, heads=heads, batch=batch, n_t=n_t, n_dec=n_dec),
        grid=(batch + 1, max(n_t, n_dec)),
        in_specs=[hspec(0), hspec(1), hspec(2), hspec(3),
                  pl.BlockSpec((3, width), lambda b, t: (0, 0)),
                  pl.BlockSpec((1, HG_HEAD_DIM), lambda b, t: (0, 0)),
                  s_in_spec],
        out_specs=[pl.BlockSpec((MIX_TILE, width), lambda b, t: (row_blk(b, t), 0)),
                   pl.BlockSpec((1, heads, HG_HEAD_DIM, HG_HEAD_DIM),
                                lambda b, t: (jnp.minimum(b, batch - 1), 0, 0, 0)),
                   s_spec],
        out_shape=[jax.ShapeDtypeStruct((m, width), F32),
                   jax.ShapeDtypeStruct((batch, heads, HG_HEAD_DIM, HG_HEAD_DIM), F32),
                   jax.ShapeDtypeStruct(state.shape[1:], F32)],
        scratch_shapes=[pltpu.VMEM((heads, HG_HEAD_DIM, HG_HEAD_DIM), F32)],
        compiler_params=pltpu.CompilerParams(dimension_semantics=("arbitrary", "arbitrary")),
        name="hgrn",
    )(h_all, h_all, h_all, h_all, lb_rows, norm_w, state)


def _rglru_terms(xc, gate_z, wa_ref, ba_ref, wx_ref, bx_ref, lam_ref):
    r = _sigmoid(_dot(xc, wa_ref[...]) + ba_ref[...])
    i = _sigmoid(_dot(xc, wx_ref[...]) + bx_ref[...])
    lam = lam_ref[...]
    log_sig_lam = jnp.minimum(lam, 0.0) - jnp.log1p(jnp.exp(-jnp.abs(lam)))
    log_a = RG_C * r * log_sig_lam
    a = jnp.exp(log_a)
    one_m_a2 = -jnp.tanh(log_a) * (a * a + 1.0)
    bterm = jnp.sqrt(one_m_a2) * (i * xc)
    return a, bterm, _gelu_tanh(gate_z)


def _rg_pool_prompt_tile(rx_ref, rgate_ref, pin_ref, w, o_rg_ref, o_pl_ref, h_out_ref, ctail_ref, ptail_ref,
                         rx_ext, pin_ext, h_carry, t_idx):
    cw_ref, cb_ref, wa_ref, ba_ref, wx_ref, bx_ref, lam_ref, pw_ref, ps_ref = w
    tile = MIX_TILE
    width = rx_ref.shape[1]
    conv_pad = SUBLANES
    pool_pad = 2 * SUBLANES

    @pl.when(t_idx == 0)
    def _():
        rx_ext[0:conv_pad, :] = jnp.zeros((conv_pad, width), F32)
        pin_ext[0:pool_pad, :] = jnp.zeros((pool_pad, width), F32)
        h_carry[...] = jnp.zeros_like(h_carry)

    x = rx_ref[...]
    rx_ext[conv_pad:conv_pad + tile, :] = x
    n_taps = cw_ref.shape[0]
    xc = x * cw_ref[n_taps - 1:n_taps, :] + cb_ref[...]
    for j in range(1, n_taps):
        xc = xc + rx_ext[conv_pad - j:conv_pad - j + tile, :] * cw_ref[n_taps - 1 - j:n_taps - j, :]
    tail = rx_ext[tile:tile + conv_pad, :]
    rx_ext[0:conv_pad, :] = tail
    ctail_ref[0] = tail

    a, bterm, gate = _rglru_terms(xc, rgate_ref[...], wa_ref, ba_ref, wx_ref, bx_ref, lam_ref)

    rid = lax.broadcasted_iota(I32, (tile, LANES), 0)
    for blk in range(width // LANES):
        sl = slice(blk * LANES, (blk + 1) * LANES)
        a_c, b_c = a[:, sl], bterm[:, sl]
        s = 1
        while s < tile:
            a_sh = jnp.where(rid >= s, pltpu.roll(a_c, s, axis=0), 1.0)
            b_sh = jnp.where(rid >= s, pltpu.roll(b_c, s, axis=0), 0.0)
            b_c = a_c * b_sh + b_c
            a_c = a_c * a_sh
            s *= 2
        hh = a_c * h_carry[:, sl] + b_c
        h_carry[:, sl] = hh[tile - 1:tile, :]
        o_rg_ref[:, sl] = hh * gate[:, sl]

    h_out_ref[0] = h_carry[...]

    p = pin_ref[...]
    pin_ext[pool_pad:pool_pad + tile, :] = p
    pos1 = t_idx * tile + rid + 1
    group = width // len(POOL_WINDOWS)
    pooled = []
    for g, win_len in enumerate(POOL_WINDOWS):
        sl = slice(g * group, (g + 1) * group)
        win = p[:, sl]
        for j in range(1, win_len):
            win = win + pin_ext[pool_pad - j:pool_pad - j + tile, sl]
        count = jnp.minimum(pos1, win_len).astype(F32)
        pooled.append(win / count - p[:, sl])
    ptail = pin_ext[tile:tile + pool_pad, :]
    pin_ext[0:pool_pad, :] = ptail
    ptail_ref[0] = ptail
    o_pl_ref[...] = _dot(jnp.concatenate(pooled, axis=1), pw_ref[...]) * ps_ref[...]


def _rg_pool_decode_rows(rx_ref, rgate_ref, pin_ref, cbuf_ref, h0_ref, pbuf_ref, w, o_rg_ref, o_pl_ref, h_out_ref):
    cw_ref, cb_ref, wa_ref, ba_ref, wx_ref, bx_ref, lam_ref, pw_ref, ps_ref = w
    x = rx_ref[...]
    n_taps = cw_ref.shape[0]
    xc = x * cw_ref[n_taps - 1:n_taps, :] + cb_ref[...]
    for j in range(n_taps - 1):
        xc = xc + cbuf_ref[j] * cw_ref[j:j + 1, :]
    a, bterm, gate = _rglru_terms(xc, rgate_ref[...], wa_ref, ba_ref, wx_ref, bx_ref, lam_ref)
    hh = a * h0_ref[...] + bterm
    h_out_ref[...] = hh
    o_rg_ref[...] = hh * gate

    p = pin_ref[...]
    n_buf = pbuf_ref.shape[0]
    width = p.shape[1]
    group = width // len(POOL_WINDOWS)
    pooled = []
    for g, win_len in enumerate(POOL_WINDOWS):
        sl = slice(g * group, (g + 1) * group)
        win = p[:, sl]
        for j in range(1, win_len):
            win = win + pbuf_ref[n_buf - j, :, sl]
        pooled.append(win / float(min(PAST_LEN + 1, win_len)) - p[:, sl])
    o_pl_ref[...] = _dot(jnp.concatenate(pooled, axis=1), pw_ref[...]) * ps_ref[...]


def _rg_pool_kernel(rx_ref, rgate_ref, pin_ref, cbuf_ref, h0_ref, pbuf_ref,
                    cw_ref, cb_ref, wa_ref, ba_ref, wx_ref, bx_ref, lam_ref, pw_ref, ps_ref,
                    o_rg_ref, o_pl_ref, h_p_ref, ctail_ref, ptail_ref, h_s_ref,
                    rx_ext, pin_ext, h_carry, *, batch):
    b = pl.program_id(0)
    t = pl.program_id(1)
    w = (cw_ref, cb_ref, wa_ref, ba_ref, wx_ref, bx_ref, lam_ref, pw_ref, ps_ref)

    @pl.when(b < batch)
    def _():
        _rg_pool_prompt_tile(rx_ref, rgate_ref, pin_ref, w, o_rg_ref, o_pl_ref, h_p_ref, ctail_ref, ptail_ref,
                             rx_ext, pin_ext, h_carry, t)

    @pl.when(jnp.logical_and(b == batch, t == 0))
    def _():
        _rg_pool_decode_rows(rx_ref, rgate_ref, pin_ref, cbuf_ref, h0_ref, pbuf_ref, w, o_rg_ref, o_pl_ref, h_s_ref)


def _rg_pool(h_all, cbuf_t, h0, pbuf_t, rg_w, pool_w, batch, seq, col0):
    n_s, width = h0.shape
    assert n_s == MIX_TILE, "the sample group must fill exactly one mixer tile"
    n_t = seq // MIX_TILE
    c0 = col0 // width
    m = batch * seq + n_s

    def row_blk(b, t):
        return jnp.where(b < batch, b * n_t + t, batch * n_t)

    def prompt_blk(b):
        return jnp.minimum(b, batch - 1)

    def hspec(j):
        return pl.BlockSpec((MIX_TILE, width), lambda b, t: (row_blk(b, t), c0 + j))

    def full(arr):
        return pl.BlockSpec(arr.shape, lambda b, t: (0,) * arr.ndim)

    weights = [rg_w["conv_w"], rg_w["conv_b"], rg_w["wa"], rg_w["ba"], rg_w["wx"], rg_w["bx"], rg_w["lam"],
               pool_w["w"], pool_w["scale"]]
    o_spec = pl.BlockSpec((MIX_TILE, width), lambda b, t: (row_blk(b, t), 0))
    return pl.pallas_call(
        functools.partial(_rg_pool_kernel, batch=batch),
        grid=(batch + 1, n_t),
        in_specs=[hspec(0), hspec(1), hspec(2), full(cbuf_t), full(h0), full(pbuf_t)] + [full(w) for w in weights],
        out_specs=[o_spec, o_spec,
                   pl.BlockSpec((1, 1, width), lambda b, t: (prompt_blk(b), 0, 0)),
                   pl.BlockSpec((1, SUBLANES, width), lambda b, t: (prompt_blk(b), 0, 0)),
                   pl.BlockSpec((1, 2 * SUBLANES, width), lambda b, t: (prompt_blk(b), 0, 0)),
                   pl.BlockSpec((n_s, width), lambda b, t: (0, 0))],
        out_shape=[jax.ShapeDtypeStruct((m, width), F32), jax.ShapeDtypeStruct((m, width), F32),
                   jax.ShapeDtypeStruct((batch, 1, width), F32),
                   jax.ShapeDtypeStruct((batch, SUBLANES, width), F32),
                   jax.ShapeDtypeStruct((batch, 2 * SUBLANES, width), F32),
                   jax.ShapeDtypeStruct((n_s, width), F32)],
        scratch_shapes=[pltpu.VMEM((MIX_TILE + SUBLANES, width), F32),
                        pltpu.VMEM((MIX_TILE + 2 * SUBLANES, width), F32),
                        pltpu.VMEM((1, width), F32)],
        compiler_params=pltpu.CompilerParams(dimension_semantics=("arbitrary", "arbitrary")),
        name="rg_pool",
    )(h_all, h_all, h_all, cbuf_t, h0, pbuf_t, *weights)


def _post_mix_kernel(x_ref, o_hg_ref, o_rg_ref, o_pl_ref, w_ref, g_ref, b_ref, rwh_ref, rwl_ref, rb_ref,
                     x1_ref, idx_ref, gate_ref, *, alpha, n_experts):
    n_hg = o_hg_ref.shape[1]
    n_rg = o_rg_ref.shape[1]
    y = _dot(o_hg_ref[...], w_ref[0:n_hg, :])
    y = y + _dot(o_rg_ref[...], w_ref[n_hg:n_hg + n_rg, :])
    y = y + _dot(o_pl_ref[...], w_ref[n_hg + n_rg:, :])
    x1 = _layer_norm(alpha * x_ref[...] + y, g_ref[...], b_ref[...])
    x1_ref[...] = x1

    x_hi = x1.astype(BF16)
    x_lo = (x1 - x_hi.astype(F32)).astype(BF16)
    logits = (jnp.dot(x_hi, rwh_ref[...], preferred_element_type=F32)
              + jnp.dot(x_lo, rwh_ref[...], preferred_element_type=F32)
              + jnp.dot(x_hi, rwl_ref[...], preferred_element_type=F32)) + rb_ref[...]

    lane = lax.broadcasted_iota(I32, logits.shape, 1).astype(F32)
    vals = jnp.where(lane < n_experts, logits, NEG_BIG)
    idx_out = jnp.zeros(logits.shape, F32)
    exp_out = jnp.zeros(logits.shape, F32)
    v0 = None
    denom = None
    for k in range(TOP_K):
        m = jnp.max(vals, axis=-1, keepdims=True)
        idx = jnp.min(jnp.where(vals == m, lane, float(LANES)), axis=-1, keepdims=True)
        if k == 0:
            v0 = m
        e = jnp.exp(m - v0)
        denom = e if denom is None else denom + e
        idx_out = jnp.where(lane == k, idx, idx_out)
        exp_out = jnp.where(lane == k, e, exp_out)
        vals = jnp.where(lane == idx, 2.0 * NEG_BIG, vals)
    idx_ref[...] = idx_out.astype(I32)
    gate_ref[...] = exp_out / denom


def _post_mix(x, o_hg, o_rg, o_pl, w_out_bf16, layer, ln_g, ln_b, rw_hi, rw_lo, rb_pad, alpha, n_experts):
    m, d = x.shape

    def rows(arr):
        return pl.BlockSpec((POST_TM, arr.shape[1]), lambda i: (i, 0))

    def full(arr):
        return pl.BlockSpec(arr.shape, lambda i: (0,) * arr.ndim)

    w_spec = pl.BlockSpec((None,) + w_out_bf16.shape[1:], lambda i: (layer, 0, 0))
    return pl.pallas_call(
        functools.partial(_post_mix_kernel, alpha=alpha, n_experts=n_experts),
        grid=(m // POST_TM,),
        in_specs=[rows(x), rows(o_hg), rows(o_rg), rows(o_pl), w_spec, full(ln_g), full(ln_b),
                  full(rw_hi), full(rw_lo), full(rb_pad)],
        out_specs=[pl.BlockSpec((POST_TM, d), lambda i: (i, 0)),
                   pl.BlockSpec((POST_TM, LANES), lambda i: (i, 0)),
                   pl.BlockSpec((POST_TM, LANES), lambda i: (i, 0))],
        out_shape=[jax.ShapeDtypeStruct((m, d), F32), jax.ShapeDtypeStruct((m, LANES), I32),
                   jax.ShapeDtypeStruct((m, LANES), F32)],
        compiler_params=pltpu.CompilerParams(dimension_semantics=("parallel",),
                                             vmem_limit_bytes=VMEM_BYTES_V7X * 3 // 4),
        name="post_mix",
    )(x, o_hg, o_rg, o_pl, w_out_bf16, ln_g, ln_b, rw_hi, rw_lo, rb_pad)


def _for_rows(lo, hi, fn):
    g_lo = lo >> SUBLANE_SHIFT
    g_hi = hi >> SUBLANE_SHIFT

    def group_body(g, carry):
        for u in range(SUBLANES):
            fn(g, u)
        return carry

    def single_body(r, carry):
        fn(g_hi, r - (g_hi << SUBLANE_SHIFT))
        return carry

    lax.fori_loop(g_lo, g_hi, group_body, 0)
    lax.fori_loop(g_hi << SUBLANE_SHIFT, hi, single_body, 0)


def _moe_kernel(item_expert, item_rows, item_start, order_hbm, x_hbm, wgu_ref, bgu_ref, wd_ref, bd_ref, y_hbm,
                acc_ref, stage_ref, xg_ref, idx_ref, ctr_ref, sem_idx, sem_in, sem_out, *, n_tokens, n_chunks):
    del item_expert
    w = pl.program_id(0)
    c = pl.program_id(1)
    n_items = pl.num_programs(0)
    last_c = n_chunks - 1
    n = item_rows[w]
    n_blk = (n + MOE_BLK - 1) // MOE_BLK
    d = xg_ref.shape[1]
    tn = wgu_ref.shape[1]
    blk_groups = MOE_BLK // SUBLANES
    nxt = jnp.minimum(w + 1, n_items - 1)
    n_next = jnp.where(w + 1 < n_items, item_rows[nxt], 0)
    this_win = pl.multiple_of((w & 1) * MOE_IDX_LEN, SMEM_1D_TILE)
    next_win = pl.multiple_of((1 - (w & 1)) * MOE_IDX_LEN, SMEM_1D_TILE)

    def window_offset(item):
        return item_start[item] & (SMEM_1D_TILE - 1)

    def window_copy(item, win):
        src0 = pl.multiple_of(item_start[item] - window_offset(item), SMEM_1D_TILE)
        return pltpu.make_async_copy(order_hbm.at[pl.ds(src0, MOE_IDX_LEN)], idx_ref.at[pl.ds(win, MOE_IDX_LEN)],
                                     sem_idx)

    def gather_row(first, last_row, g, u):
        row = jnp.minimum((g << SUBLANE_SHIFT) + u, last_row)
        pair = idx_ref[first + row]
        pltpu.make_async_copy(x_hbm.at[pl.ds(pair >> TOP_K_SHIFT, 1)], stage_ref.at[g, pl.ds(u, 1)],
                              sem_in).start()

    def start_gather(item, win, lo, hi):
        first = win + window_offset(item)
        last_row = jnp.maximum(item_rows[item] - 1, 0)
        _for_rows(lo, hi, functools.partial(gather_row, first, last_row))

    def start_scatter(lo, hi):
        first = this_win + window_offset(w)

        def issue(g, u):
            pair = idx_ref[first + (g << SUBLANE_SHIFT) + u]
            dst = (pair & (TOP_K - 1)) * n_tokens + (pair >> TOP_K_SHIFT)
            pltpu.make_async_copy(acc_ref.at[g, pl.ds(u, 1)], y_hbm.at[pl.ds(dst, 1)], sem_out).start()
        _for_rows(lo, hi, issue)

    def wait_gather(count):
        _for_rows(0, count, lambda g, u: pltpu.make_async_copy(
            x_hbm.at[pl.ds(0, 1)], stage_ref.at[0, pl.ds(0, 1)], sem_in).wait())

    def wait_scatter(count):
        _for_rows(0, count, lambda g, u: pltpu.make_async_copy(
            acc_ref.at[0, pl.ds(0, 1)], y_hbm.at[pl.ds(0, 1)], sem_out).wait())

    @pl.when(jnp.logical_and(w == 0, c == 0))
    def _():
        stage_ref[...] = jnp.zeros_like(stage_ref)

        @pl.when(n > 0)
        def _():
            window_copy(0, this_win).start()
            window_copy(0, this_win).wait()
            start_gather(0, this_win, 0, n)
            ctr_ref[1] = n

    @pl.when(n > 0)
    def _():
        @pl.when(c == 0)
        def _():
            window_copy(nxt, next_win).start()
            wait_gather(ctr_ref[1])

            @pl.when(w > 0)
            def _():
                wait_scatter(item_rows[jnp.maximum(w - 1, 0)])

            def cast_body(i, carry):
                gs = pl.ds(pl.multiple_of(i * blk_groups, blk_groups), blk_groups)
                rs = pl.ds(pl.multiple_of(i * MOE_BLK, MOE_BLK), MOE_BLK)
                xg_ref[rs, :] = stage_ref[gs].reshape(MOE_BLK, d).astype(BF16)
                acc_ref[gs] = jnp.broadcast_to(bd_ref[...], (blk_groups, SUBLANES, d))
                return carry
            lax.fori_loop(0, n_blk, cast_body, 0)
            window_copy(nxt, next_win).wait()
            ctr_ref[0] = 0

        next_first = next_win + window_offset(nxt)
        next_last_row = jnp.maximum(n_next - 1, 0)

        sel_r = lax.broadcasted_iota(I32, (tn, tn // 2), 0)
        sel_c = lax.broadcasted_iota(I32, (tn, tn // 2), 1)
        pick = jnp.where(sel_r == 2 * sel_c, 1.0, 0.0).astype(BF16)

        def sub_tile(row0, rows):
            groups = rows // SUBLANES
            gs = pl.ds(pl.multiple_of(row0 >> SUBLANE_SHIFT, blk_groups), groups)
            h = jnp.dot(xg_ref[pl.ds(row0, rows), :], wgu_ref[...].astype(BF16),
                        preferred_element_type=F32) + bgu_ref[...]
            h_next = pltpu.roll(h, tn - 1, axis=1)
            gate = jnp.minimum(h, SWIGLU_LIMIT)
            up = jnp.clip(h_next, -SWIGLU_LIMIT, SWIGLU_LIMIT)
            act = (up + 1.0) * gate * _sigmoid(gate * SWIGLU_ALPHA)
            act_dense = jnp.dot(act.astype(BF16), pick, preferred_element_type=F32)
            upd = jnp.dot(act_dense.astype(BF16), wd_ref[...].astype(BF16), preferred_element_type=F32)
            acc_ref[gs] = acc_ref[gs] + upd.reshape(groups, SUBLANES, d)

            issued = ctr_ref[0]
            g_base = issued >> SUBLANE_SHIFT
            for k in range(rows // n_chunks):
                gather_row(next_first, next_last_row, g_base + k // SUBLANES, k % SUBLANES)
            ctr_ref[0] = issued + rows // n_chunks

            @pl.when(c == last_c)
            def _():
                start_scatter(row0, jnp.minimum(row0 + rows, n))

        big = MOE_SUB_ROWS[0]
        big_blocks = big // MOE_BLK
        left = n_blk & (big_blocks - 1)
        merge = jnp.logical_and(left == 1, n_blk > big_blocks)
        n_big = n_blk // big_blocks - merge.astype(I32)

        def big_body(i, carry):
            sub_tile(pl.multiple_of(i * big, big), big)
            return carry
        lax.fori_loop(0, n_big, big_body, 0)
        row = n_big * big

        @pl.when(merge)
        def _(row=row):
            sub_tile(pl.multiple_of(row, MOE_BLK), big + MOE_BLK)
        row = row + jnp.where(merge, big + MOE_BLK, 0)
        left = jnp.where(merge, 0, left)
        for rows in MOE_SUB_ROWS[1:]:
            take = (left & (rows // MOE_BLK)) != 0

            @pl.when(take)
            def _(row=row, rows=rows):
                sub_tile(pl.multiple_of(row, MOE_BLK), rows)
            row = row + jnp.where(take, rows, 0)

        @pl.when(c == last_c)
        def _():
            issued = ctr_ref[0]

            @pl.when(issued < n_next)
            def _():
                start_gather(nxt, next_win, issued, n_next)
            total = jnp.maximum(issued, n_next)
            ctr_ref[1] = total

            @pl.when(n_next == 0)
            def _():
                wait_scatter(n)
                wait_gather(total)


def _moe(x1, tables, wgu, bgu, wd, bd, layer, n_items):
    m, d = x1.shape
    depth, n_exp, _, n_gu = wgu.shape
    n_chunks = n_gu // MOE_TN
    half = MOE_TN // 2
    row_groups = MOE_TM // SUBLANES

    def chunk(c, rows_ref, w):
        return jnp.where(rows_ref[w] > 0, c, n_chunks - 1)

    grid_spec = pltpu.PrefetchScalarGridSpec(
        num_scalar_prefetch=3,
        grid=(n_items, n_chunks),
        in_specs=[
            pl.BlockSpec(memory_space=pl.ANY),
            pl.BlockSpec(memory_space=pl.ANY),
            pl.BlockSpec((None, None, d, MOE_TN), lambda w, c, ie, ir, ist: (layer, ie[w], 0, chunk(c, ir, w))),
            pl.BlockSpec((None, None, 1, MOE_TN), lambda w, c, ie, ir, ist: (layer, ie[w], 0, chunk(c, ir, w))),
            pl.BlockSpec((None, None, half, d), lambda w, c, ie, ir, ist: (layer, ie[w], chunk(c, ir, w), 0)),
            pl.BlockSpec((None, None, 1, d), lambda w, c, ie, ir, ist: (layer, ie[w], 0, 0)),
        ],
        out_specs=pl.BlockSpec(memory_space=pl.ANY),
        scratch_shapes=[pltpu.VMEM((row_groups, SUBLANES, d), F32), pltpu.VMEM((row_groups, SUBLANES, d), F32),
                        pltpu.VMEM((MOE_TM, d), BF16),
                        pltpu.SMEM((2 * MOE_IDX_LEN,), I32), pltpu.SMEM((2,), I32),
                        pltpu.SemaphoreType.DMA, pltpu.SemaphoreType.DMA, pltpu.SemaphoreType.DMA],
    )
    assert all((rows // n_chunks) % SUBLANES == 0 for rows in MOE_SUB_ROWS + (MOE_SUB_ROWS[0] + MOE_BLK,))
    return pl.pallas_call(
        functools.partial(_moe_kernel, n_tokens=m, n_chunks=n_chunks),
        grid_spec=grid_spec,
        out_shape=jax.ShapeDtypeStruct((TOP_K * m, d), F32),
        compiler_params=pltpu.CompilerParams(dimension_semantics=("arbitrary", "arbitrary"),
                                             vmem_limit_bytes=VMEM_BYTES_V7X * 7 // 8),
        name="moe_experts",
    )(tables["expert"], tables["rows"], tables["start"], tables["order"], x1,
      wgu, bgu.reshape(depth, n_exp, 1, n_gu), wd, bd.reshape(depth, n_exp, 1, d))


def _route_tables(top_idx, n_experts, n_items):
    m = top_idx.shape[0]
    n_pairs = m * TOP_K
    pair_expert = top_idx.reshape(-1)
    order = jnp.argsort(pair_expert, stable=True).astype(I32)
    counts = jnp.sum((pair_expert[:, None] == jnp.arange(n_experts, dtype=I32)[None, :]).astype(I32), axis=0)
    row_end = jnp.cumsum(counts)
    row_off = row_end - counts
    tiles = (counts + MOE_TM - 1) // MOE_TM
    tile_end = jnp.cumsum(tiles)
    tile_off = tile_end - tiles
    total = tile_end[-1]
    w = jnp.arange(n_items, dtype=I32)
    e_w = jnp.minimum(jnp.searchsorted(tile_end, w, side="right").astype(I32), n_experts - 1)
    j = w - tile_off[e_w]
    active = w < total
    start = jnp.where(active, row_off[e_w] + j * MOE_TM, 0)
    rows = jnp.where(active, jnp.clip(counts[e_w] - j * MOE_TM, 0, MOE_TM), 0)
    e_w = jnp.where(active, e_w, e_w[jnp.maximum(total - 1, 0)])
    padded = -(-n_pairs // SMEM_1D_TILE) * SMEM_1D_TILE + MOE_IDX_LEN
    return {"expert": e_w.astype(I32), "rows": rows.astype(I32), "start": start.astype(I32),
            "order": jnp.pad(order, (0, padded - n_pairs))}


def _combine_kernel(x1_ref, gate_ref, y_ref, g_ref, b_ref, o_ref, *, alpha):
    gates = gate_ref[...]
    acc = alpha * x1_ref[...]
    for k in range(TOP_K):
        acc = acc + gates[:, k:k + 1] * y_ref[k]
    o_ref[...] = _layer_norm(acc, g_ref[...], b_ref[...])


def _combine(x1, gates, y, ln_g, ln_b, alpha):
    m, d = x1.shape
    return pl.pallas_call(
        functools.partial(_combine_kernel, alpha=alpha),
        grid=(m // COMBINE_TM,),
        in_specs=[pl.BlockSpec((COMBINE_TM, d), lambda i: (i, 0)),
                  pl.BlockSpec((COMBINE_TM, LANES), lambda i: (i, 0)),
                  pl.BlockSpec((TOP_K, COMBINE_TM, d), lambda i: (0, i, 0)),
                  pl.BlockSpec((1, d), lambda i: (0, 0)),
                  pl.BlockSpec((1, d), lambda i: (0, 0))],
        out_specs=pl.BlockSpec((COMBINE_TM, d), lambda i: (i, 0)),
        out_shape=jax.ShapeDtypeStruct((m, d), F32),
        compiler_params=pltpu.CompilerParams(dimension_semantics=("parallel",),
                                             vmem_limit_bytes=VMEM_BYTES_V7X * 3 // 4),
        name="moe_combine",
    )(x1, gates, y.reshape(TOP_K, m, d), ln_g, ln_b)


def _block_diag(w):
    nb, n, _ = w.shape
    eye = jnp.eye(nb, dtype=w.dtype)
    return (eye[:, None, :, None] * w[:, :, None, :]).reshape(nb * n, nb * n)


def _row(v):
    return v.reshape(1, -1).astype(F32)


def kernel(x_prompt, x_sample, state_hgrn, state_rglru_h, state_rglru_conv, state_pool, w_in, hg_lb_logits,
           hg_norm_w, rg_conv_w, rg_conv_b, rg_wa, rg_ba, rg_wx, rg_bx, rg_lambda, pool_w, pool_scale, w_out,
           ln1_g, ln1_b, router_w, router_b, moe_w_gate_up, moe_b_gate_up, moe_w_down, moe_b_down, ln2_g, ln2_b):
    depth = w_in.shape[0]
    bp, lp, d_model = x_prompt.shape
    bs = x_sample.shape[0]
    n_p = bp * lp
    m = n_p + bs
    hg_width = hg_lb_logits.shape[1]
    rg_width = rg_lambda.shape[1]
    n_experts = router_w.shape[2]
    n_conv = state_rglru_conv.shape[2]
    n_pool = state_pool.shape[2]
    alpha = float((2 * depth) ** 0.25)
    n_items = n_experts + (m * TOP_K) // MOE_TM
    rg_col0 = 4 * hg_width

    p_lb = jax.nn.softmax(hg_lb_logits.astype(F32), axis=0)
    lbs = jnp.cumsum(p_lb, axis=0)
    lbs = lbs - lbs[0]

    w_in_bf16 = w_in.astype(BF16)
    w_out_bf16 = w_out.astype(BF16)
    x = jnp.concatenate([x_prompt.reshape(n_p, d_model), x_sample.reshape(bs, d_model)], axis=0)
    outs = {k: [] for k in ("hg_p", "hg_s", "h_p", "h_s", "c_p", "c_s", "pl_p", "pl_s")}
    for l in range(depth):
        lb = lbs[l]
        lb_rows = jnp.stack([jnp.log(lb), jnp.log1p(-lb), 1.0 - lb], axis=0)
        rg_w = {"conv_w": rg_conv_w[l], "conv_b": _row(rg_conv_b[l]), "wa": _block_diag(rg_wa[l]).astype(BF16),
                "ba": _row(rg_ba[l]), "wx": _block_diag(rg_wx[l]).astype(BF16), "bx": _row(rg_bx[l]),
                "lam": _row(rg_lambda[l])}
        pl_w = {"w": _block_diag(pool_w[l]).astype(BF16), "scale": _row(pool_scale[l])}

        h_all = _in_proj(x, w_in_bf16, l)
        o_hg, s_p, s_s = _hgrn(h_all, lb_rows, _row(hg_norm_w[l]), state_hgrn, l, bp, lp)
        o_rg, o_pl, h_p, c_tail, p_tail, h_s = _rg_pool(
            h_all, jnp.swapaxes(state_rglru_conv[l], 0, 1), state_rglru_h[l], jnp.swapaxes(state_pool[l], 0, 1),
            rg_w, pl_w, bp, lp, rg_col0)

        rx_s = h_all[n_p:, rg_col0:rg_col0 + rg_width]
        pin_s = h_all[n_p:, rg_col0 + 2 * rg_width:]
        outs["hg_p"].append(s_p)
        outs["hg_s"].append(s_s)
        outs["h_p"].append(h_p.reshape(bp, rg_width))
        outs["h_s"].append(h_s)
        outs["c_p"].append(c_tail[:, SUBLANES - n_conv:])
        outs["c_s"].append(jnp.concatenate([state_rglru_conv[l][:, 1:], rx_s[:, None, :]], axis=1))
        outs["pl_p"].append(p_tail[:, 2 * SUBLANES - n_pool:])
        outs["pl_s"].append(jnp.concatenate([state_pool[l][:, 1:], pin_s[:, None, :]], axis=1))

        rw = jnp.pad(router_w[l].astype(F32), ((0, 0), (0, LANES - n_experts)))
        rw_hi = rw.astype(BF16)
        rw_lo = (rw - rw_hi.astype(F32)).astype(BF16)
        rb_pad = jnp.pad(router_b[l].astype(F32), (0, LANES - n_experts)).reshape(1, LANES)
        x1, top_idx, gates = _post_mix(x, o_hg, o_rg, o_pl, w_out_bf16, l, _row(ln1_g[l]), _row(ln1_b[l]),
                                       rw_hi, rw_lo, rb_pad, alpha, n_experts)

        tables = _route_tables(top_idx[:, :TOP_K], n_experts, n_items)
        y = _moe(x1, tables, moe_w_gate_up, moe_b_gate_up, moe_w_down, moe_b_down, l, n_items)
        x = _combine(x1, gates, y, _row(ln2_g[l]), _row(ln2_b[l]), alpha)

    return (x[:n_p].reshape(bp, lp, d_model), x[n_p:].reshape(bs, 1, d_model),
            jnp.stack(outs["hg_p"]), jnp.stack(outs["hg_s"]), jnp.stack(outs["h_p"]), jnp.stack(outs["h_s"]),
            jnp.stack(outs["c_p"]), jnp.stack(outs["c_s"]), jnp.stack(outs["pl_p"]), jnp.stack(outs["pl_s"]))
```

```python
import functools

import jax
import jax.numpy as jnp
from jax import lax
from jax.experimental import pallas as pl
from jax.experimental.pallas import tpu as pltpu

F32 = jnp.float32
BF16 = jnp.bfloat16
I32 = jnp.int32

HG_HEAD_DIM = 128
RG_C = 8.0
POOL_WINDOWS = (2, 4, 8, 16)
TOP_K = 4
TOP_K_SHIFT = TOP_K.bit_length() - 1
SWIGLU_LIMIT = 7.0
SWIGLU_ALPHA = 1.702
LN_EPS = 1e-5
RMS_EPS = 1e-6
PAST_LEN = 16384
GELU_C0 = 0.7978845608028654
GELU_C1 = 0.044715

LANES = 128
SUBLANES = 8
SUBLANE_SHIFT = SUBLANES.bit_length() - 1
VMEM_BYTES_V7X = 64 * 1024 * 1024
SMEM_1D_TILE = 1024

PROJ_TM = 832
PROJ_TN = 1408
POST_TM = 320
MIX_TILE = 128
DEC_ROWS = 8
MOE_TM = 1536
MOE_BLK = 128
MOE_SUB_ROWS = (512, 256, 128)
MOE_TN = 512
MOE_IDX_LEN = -(-(MOE_TM + SMEM_1D_TILE - 1) // SMEM_1D_TILE) * SMEM_1D_TILE
COMBINE_TM = 320
NEG_BIG = -1e30

assert TOP_K == 1 << TOP_K_SHIFT


def _sigmoid(x):
    return jax.nn.sigmoid(x)


def _silu(x):
    return x * _sigmoid(x)


def _gelu_tanh(x):
    return 0.5 * x * (1.0 + jnp.tanh(GELU_C0 * (x + GELU_C1 * (x * x * x))))


def _dot(a, b):
    return jnp.dot(a.astype(BF16), b.astype(BF16), preferred_element_type=F32)


def _dot_nt(a, b):
    return lax.dot_general(a.astype(BF16), b.astype(BF16), (((1,), (1,)), ((), ())),
                           preferred_element_type=F32)


def _dot_tn(a, b):
    return lax.dot_general(a.astype(BF16), b.astype(BF16), (((0,), (0,)), ((), ())),
                           preferred_element_type=F32)


def _layer_norm(x, g, b):
    mu = jnp.mean(x, axis=-1, keepdims=True)
    xc = x - mu
    var = jnp.mean(xc * xc, axis=-1, keepdims=True)
    return xc * lax.rsqrt(var + LN_EPS) * g + b


def _in_proj_kernel(x_ref, w_ref, o_ref):
    o_ref[...] = jnp.dot(x_ref[...].astype(BF16), w_ref[...], preferred_element_type=F32)


def _in_proj(x, w_bf16, layer):
    m, k = x.shape
    n = w_bf16.shape[2]
    return pl.pallas_call(
        _in_proj_kernel,
        grid=(m // PROJ_TM, n // PROJ_TN),
        in_specs=[pl.BlockSpec((PROJ_TM, k), lambda i, j: (i, 0)),
                  pl.BlockSpec((None, k, PROJ_TN), lambda i, j: (layer, 0, j))],
        out_specs=pl.BlockSpec((PROJ_TM, PROJ_TN), lambda i, j: (i, j)),
        out_shape=jax.ShapeDtypeStruct((m, n), F32),
        compiler_params=pltpu.CompilerParams(dimension_semantics=("parallel", "arbitrary"),
                                             vmem_limit_bytes=VMEM_BYTES_V7X * 3 // 4),
        name="in_proj",
    )(x, w_bf16)


def _hgrn_gates(z, log_lb, log1m_lb, one_m_lb):
    e = jnp.exp(-jnp.abs(z))
    log_sig = jnp.minimum(z, 0.0) - jnp.log1p(e)
    c = log1m_lb + log_sig
    logf = jnp.maximum(log_lb, c) + jnp.log1p(jnp.exp(-jnp.abs(log_lb - c)))
    k = one_m_lb * (jnp.where(z > 0, e, 1.0) / (1.0 + e))
    return logf, k


def _cumsum_rows(x, rid):
    n = x.shape[0]
    s = 1
    while s < n:
        x = x + jnp.where(rid >= s, pltpu.roll(x, s, axis=0), 0.0)
        s *= 2
    return x


def _hgrn_prompt_tile(hq_ref, hf_ref, hi_ref, hg_ref, lb_ref, nw_ref, o_ref, s_out_ref, st_ref,
                      t_idx, last_t, heads):
    tile, hd = MIX_TILE, HG_HEAD_DIM
    half, quarter, mid = tile // 2, tile // 4, tile // 8

    @pl.when(t_idx == 0)
    def _():
        st_ref[...] = jnp.zeros_like(st_ref)

    row = lax.broadcasted_iota(I32, (tile, tile), 0)
    col = lax.broadcasted_iota(I32, (tile, tile), 1)
    rid = lax.broadcasted_iota(I32, (tile, hd), 0)
    mask_a = (row >= half) & (col < half)
    q_shift = quarter.bit_length() - 1
    row_q, col_q = row >> q_shift, col >> q_shift
    mask_b = (row_q == col_q + 1) & ((col_q & 1) == 0)
    mask_d = (row_q == col_q) & (col <= row)

    def head_body(h):
        sl = slice(h * hd, (h + 1) * hd)
        zq = hq_ref[:, sl]
        zf = hf_ref[:, sl]
        v = hi_ref[:, sl]
        zg = hg_ref[:, sl]
        q = _silu(zq)
        logf, k = _hgrn_gates(zf, lb_ref[0:1, sl], lb_ref[1:2, sl], lb_ref[2:3, sl])
        b = _cumsum_rows(logf, rid)

        def split_level(ref_rows):
            d = b - ref_rows
            e = jnp.exp(-jnp.abs(d))
            return q * jnp.where(d < 0, e, 1.0), k * jnp.where(d < 0, 1.0, e)

        qa, ka = split_level(b[half - 1:half, :])
        ref_b = jnp.where(rid < half, b[quarter - 1:quarter, :], b[half + quarter - 1:half + quarter, :])
        qb, kb = split_level(ref_b)
        ref_d = b[mid - 1:mid, :]
        for blk in range(1, 4):
            lo = blk * quarter
            ref_d = jnp.where(rid >= lo, b[lo + mid - 1:lo + mid, :], ref_d)
        dd = b - ref_d
        qd = q * jnp.exp(dd)
        kd = k * jnp.exp(-dd)

        scores = jnp.where(mask_a, _dot_nt(qa, ka),
                           jnp.where(mask_b, _dot_nt(qb, kb),
                                     jnp.where(mask_d, _dot_nt(qd, kd), 0.0)))
        st = st_ref[h]
        o = _dot(scores, v) + _dot_nt(q * jnp.exp(b), st)
        b_last = b[tile - 1:tile, :]
        st_new = st * jnp.exp(b_last) + _dot_tn(v, k * jnp.exp(b_last - b))
        st_ref[h] = st_new

        o = o * lax.rsqrt(jnp.mean(o * o, axis=-1, keepdims=True) + RMS_EPS) * nw_ref[...] * _silu(zg)
        o_ref[:, sl] = o

    for h in range(heads):
        head_body(h)

    @pl.when(t_idx == last_t)
    def _():
        for h in range(heads):
            s_out_ref[0, h] = st_ref[h].T


def _hgrn_decode_rows(hq_ref, hf_ref, hi_ref, hg_ref, lb_ref, nw_ref, s_ref, o_ref, s_out_ref, step, heads):
    hd = HG_HEAD_DIM
    rows = DEC_ROWS
    rid = lax.broadcasted_iota(I32, (rows, hd), 0)
    rs = pl.ds(pl.multiple_of(step * rows, rows), rows)

    for h in range(heads):
        sl = slice(h * hd, (h + 1) * hd)
        zq = hq_ref[rs, sl]
        zf = hf_ref[rs, sl]
        v = hi_ref[rs, sl]
        zg = hg_ref[rs, sl]
        q = _silu(zq)
        logf, k = _hgrn_gates(zf, lb_ref[0:1, sl], lb_ref[1:2, sl], lb_ref[2:3, sl])
        f = jnp.exp(logf)
        stack = jnp.concatenate([q, k, f, jnp.zeros((hd - 3 * rows, hd), F32)], axis=0)
        cols = stack.T
        o_rows = jnp.zeros((rows, hd), F32)
        for j in range(rows):
            q_col = cols[:, j:j + 1]
            k_col = cols[:, rows + j:rows + j + 1]
            f_col = cols[:, 2 * rows + j:2 * rows + j + 1]
            s_new = s_ref[j, h] * f_col + k_col * v[j:j + 1, :]
            s_out_ref[j, h] = s_new
            o_j = jnp.sum(s_new * q_col, axis=0, keepdims=True)
            o_rows = jnp.where(rid == j, o_j, o_rows)
        o = o_rows * lax.rsqrt(jnp.mean(o_rows * o_rows, axis=-1, keepdims=True) + RMS_EPS)
        o_ref[rs, sl] = o * nw_ref[...] * _silu(zg)


def _hgrn_kernel(hq_ref, hf_ref, hi_ref, hg_ref, lb_ref, nw_ref, s_in_ref, o_ref, s_p_ref, s_s_ref, st_ref,
                 *, heads, batch, n_t, n_dec):
    b = pl.program_id(0)
    t = pl.program_id(1)

    @pl.when(jnp.logical_and(b < batch, t < n_t))
    def _():
        _hgrn_prompt_tile(hq_ref, hf_ref, hi_ref, hg_ref, lb_ref, nw_ref, o_ref, s_p_ref, st_ref,
                          t, n_t - 1, heads)

    @pl.when(jnp.logical_and(b == batch, t < n_dec))
    def _():
        _hgrn_decode_rows(hq_ref, hf_ref, hi_ref, hg_ref, lb_ref, nw_ref, s_in_ref, o_ref, s_s_ref, t, heads)


def _hgrn(h_all, lb_rows, norm_w, state, layer, batch, seq):
    n_s, heads = state.shape[1], state.shape[2]
    assert n_s == MIX_TILE, "the sample group must fill exactly one mixer tile"
    width = heads * HG_HEAD_DIM
    n_t = seq // MIX_TILE
    n_dec = n_s // DEC_ROWS
    m = batch * seq + n_s

    def row_blk(b, t):
        return jnp.where(b < batch, b * n_t + jnp.minimum(t, n_t - 1), batch * n_t)

    def dec_blk(b, t):
        return jnp.where(b == batch, jnp.minimum(t, n_dec - 1), 0)

    def hspec(j):
        return pl.BlockSpec((MIX_TILE, width), lambda b, t: (row_blk(b, t), j))

    s_shape = (DEC_ROWS, heads, HG_HEAD_DIM, HG_HEAD_DIM)
    s_in_spec = pl.BlockSpec((None,) + s_shape, lambda b, t: (layer, dec_blk(b, t), 0, 0, 0))
    s_spec = pl.BlockSpec(s_shape, lambda b, t: (dec_blk(b, t), 0, 0, 0))
    return pl.pallas_call(
        functools.partial(_hgrn_kernel, heads=heads, batch=batch, n_t=n_t, n_dec=n_dec),
        grid=(batch + 1, max(n_t, n_dec)),
        in_specs=[hspec(0), hspec(1), hspec(2), hspec(3),
                  pl.BlockSpec((3, width), lambda b, t: (0, 0)),
                  pl.BlockSpec((1, HG_HEAD_DIM), lambda b, t: (0, 0)),
                  s_in_spec],
        out_specs=[pl.BlockSpec((MIX_TILE, width), lambda b, t: (row_blk(b, t), 0)),
                   pl.BlockSpec((1, heads, HG_HEAD_DIM, HG_HEAD_DIM),
                                lambda b, t: (jnp.minimum(b, batch - 1), 0, 0, 0)),
                   s_spec],
        out_shape=[jax.ShapeDtypeStruct((m, width), F32),
                   jax.ShapeDtypeStruct((batch, heads, HG_HEAD_DIM, HG_HEAD_DIM), F32),
                   jax.ShapeDtypeStruct(state.shape[1:], F32)],
        scratch_shapes=[pltpu.VMEM((heads, HG_HEAD_DIM, HG_HEAD_DIM), F32)],
        compiler_params=pltpu.CompilerParams(dimension_semantics=("arbitrary", "arbitrary")),
        name="hgrn",
    )(h_all, h_all, h_all, h_all, lb_rows, norm_w, state)


def _rglru_terms(xc, gate_z, wa_ref, ba_ref, wx_ref, bx_ref, lam_ref):
    r = _sigmoid(_dot(xc, wa_ref[...]) + ba_ref[...])
    i = _sigmoid(_dot(xc, wx_ref[...]) + bx_ref[...])
    lam = lam_ref[...]
    log_sig_lam = jnp.minimum(lam, 0.0) - jnp.log1p(jnp.exp(-jnp.abs(lam)))
    log_a = RG_C * r * log_sig_lam
    a = jnp.exp(log_a)
    one_m_a2 = -jnp.tanh(log_a) * (a * a + 1.0)
    bterm = jnp.sqrt(one_m_a2) * (i * xc)
    return a, bterm, _gelu_tanh(gate_z)


def _rg_pool_prompt_tile(rx_ref, rgate_ref, pin_ref, w, o_rg_ref, o_pl_ref, h_out_ref, ctail_ref, ptail_ref,
                         rx_ext, pin_ext, h_carry, t_idx):
    cw_ref, cb_ref, wa_ref, ba_ref, wx_ref, bx_ref, lam_ref, pw_ref, ps_ref = w
    tile = MIX_TILE
    width = rx_ref.shape[1]
    conv_pad = SUBLANES
    pool_pad = 2 * SUBLANES

    @pl.when(t_idx == 0)
    def _():
        rx_ext[0:conv_pad, :] = jnp.zeros((conv_pad, width), F32)
        pin_ext[0:pool_pad, :] = jnp.zeros((pool_pad, width), F32)
        h_carry[...] = jnp.zeros_like(h_carry)

    x = rx_ref[...]
    rx_ext[conv_pad:conv_pad + tile, :] = x
    n_taps = cw_ref.shape[0]
    xc = x * cw_ref[n_taps - 1:n_taps, :] + cb_ref[...]
    for j in range(1, n_taps):
        xc = xc + rx_ext[conv_pad - j:conv_pad - j + tile, :] * cw_ref[n_taps - 1 - j:n_taps - j, :]
    tail = rx_ext[tile:tile + conv_pad, :]
    rx_ext[0:conv_pad, :] = tail
    ctail_ref[0] = tail

    a, bterm, gate = _rglru_terms(xc, rgate_ref[...], wa_ref, ba_ref, wx_ref, bx_ref, lam_ref)

    rid = lax.broadcasted_iota(I32, (tile, LANES), 0)
    for blk in range(width // LANES):
        sl = slice(blk * LANES, (blk + 1) * LANES)
        a_c, b_c = a[:, sl], bterm[:, sl]
        s = 1
        while s < tile:
            a_sh = jnp.where(rid >= s, pltpu.roll(a_c, s, axis=0), 1.0)
            b_sh = jnp.where(rid >= s, pltpu.roll(b_c, s, axis=0), 0.0)
            b_c = a_c * b_sh + b_c
            a_c = a_c * a_sh
            s *= 2
        hh = a_c * h_carry[:, sl] + b_c
        h_carry[:, sl] = hh[tile - 1:tile, :]
        o_rg_ref[:, sl] = hh * gate[:, sl]

    h_out_ref[0] = h_carry[...]

    p = pin_ref[...]
    pin_ext[pool_pad:pool_pad + tile, :] = p
    pos1 = t_idx * tile + rid + 1
    group = width // len(POOL_WINDOWS)
    pooled = []
    for g, win_len in enumerate(POOL_WINDOWS):
        sl = slice(g * group, (g + 1) * group)
        win = p[:, sl]
        for j in range(1, win_len):
            win = win + pin_ext[pool_pad - j:pool_pad - j + tile, sl]
        count = jnp.minimum(pos1, win_len).astype(F32)
        pooled.append(win / count - p[:, sl])
    ptail = pin_ext[tile:tile + pool_pad, :]
    pin_ext[0:pool_pad, :] = ptail
    ptail_ref[0] = ptail
    o_pl_ref[...] = _dot(jnp.concatenate(pooled, axis=1), pw_ref[...]) * ps_ref[...]


def _rg_pool_decode_rows(rx_ref, rgate_ref, pin_ref, cbuf_ref, h0_ref, pbuf_ref, w, o_rg_ref, o_pl_ref, h_out_ref):
    cw_ref, cb_ref, wa_ref, ba_ref, wx_ref, bx_ref, lam_ref, pw_ref, ps_ref = w
    x = rx_ref[...]
    n_taps = cw_ref.shape[0]
    xc = x * cw_ref[n_taps - 1:n_taps, :] + cb_ref[...]
    for j in range(n_taps - 1):
        xc = xc + cbuf_ref[j] * cw_ref[j:j + 1, :]
    a, bterm, gate = _rglru_terms(xc, rgate_ref[...], wa_ref, ba_ref, wx_ref, bx_ref, lam_ref)
    hh = a * h0_ref[...] + bterm
    h_out_ref[...] = hh
    o_rg_ref[...] = hh * gate

    p = pin_ref[...]
    n_buf = pbuf_ref.shape[0]
    width = p.shape[1]
    group = width // len(POOL_WINDOWS)
    pooled = []
    for g, win_len in enumerate(POOL_WINDOWS):
        sl = slice(g * group, (g + 1) * group)
        win = p[:, sl]
        for j in range(1, win_len):
            win = win + pbuf_ref[n_buf - j, :, sl]
        pooled.append(win / float(min(PAST_LEN + 1, win_len)) - p[:, sl])
    o_pl_ref[...] = _dot(jnp.concatenate(pooled, axis=1), pw_ref[...]) * ps_ref[...]


def _rg_pool_kernel(rx_ref, rgate_ref, pin_ref, cbuf_ref, h0_ref, pbuf_ref,
                    cw_ref, cb_ref, wa_ref, ba_ref, wx_ref, bx_ref, lam_ref, pw_ref, ps_ref,
                    o_rg_ref, o_pl_ref, h_p_ref, ctail_ref, ptail_ref, h_s_ref,
                    rx_ext, pin_ext, h_carry, *, batch):
    b = pl.program_id(0)
    t = pl.program_id(1)
    w = (cw_ref, cb_ref, wa_ref, ba_ref, wx_ref, bx_ref, lam_ref, pw_ref, ps_ref)

    @pl.when(b < batch)
    def _():
        _rg_pool_prompt_tile(rx_ref, rgate_ref, pin_ref, w, o_rg_ref, o_pl_ref, h_p_ref, ctail_ref, ptail_ref,
                             rx_ext, pin_ext, h_carry, t)

    @pl.when(jnp.logical_and(b == batch, t == 0))
    def _():
        _rg_pool_decode_rows(rx_ref, rgate_ref, pin_ref, cbuf_ref, h0_ref, pbuf_ref, w, o_rg_ref, o_pl_ref, h_s_ref)


def _rg_pool(h_all, cbuf_t, h0, pbuf_t, rg_w, pool_w, batch, seq, col0):
    n_s, width = h0.shape
    assert n_s == MIX_TILE, "the sample group must fill exactly one mixer tile"
    n_t = seq // MIX_TILE
    c0 = col0 // width
    m = batch * seq + n_s

    def row_blk(b, t):
        return jnp.where(b < batch, b * n_t + t, batch * n_t)

    def prompt_blk(b):
        return jnp.minimum(b, batch - 1)

    def hspec(j):
        return pl.BlockSpec((MIX_TILE, width), lambda b, t: (row_blk(b, t), c0 + j))

    def full(arr):
        return pl.BlockSpec(arr.shape, lambda b, t: (0,) * arr.ndim)

    weights = [rg_w["conv_w"], rg_w["conv_b"], rg_w["wa"], rg_w["ba"], rg_w["wx"], rg_w["bx"], rg_w["lam"],
               pool_w["w"], pool_w["scale"]]
    o_spec = pl.BlockSpec((MIX_TILE, width), lambda b, t: (row_blk(b, t), 0))
    return pl.pallas_call(
        functools.partial(_rg_pool_kernel, batch=batch),
        grid=(batch + 1, n_t),
        in_specs=[hspec(0), hspec(1), hspec(2), full(cbuf_t), full(h0), full(pbuf_t)] + [full(w) for w in weights],
        out_specs=[o_spec, o_spec,
                   pl.BlockSpec((1, 1, width), lambda b, t: (prompt_blk(b), 0, 0)),
                   pl.BlockSpec((1, SUBLANES, width), lambda b, t: (prompt_blk(b), 0, 0)),
                   pl.BlockSpec((1, 2 * SUBLANES, width), lambda b, t: (prompt_blk(b), 0, 0)),
                   pl.BlockSpec((n_s, width), lambda b, t: (0, 0))],
        out_shape=[jax.ShapeDtypeStruct((m, width), F32), jax.ShapeDtypeStruct((m, width), F32),
                   jax.ShapeDtypeStruct((batch, 1, width), F32),
                   jax.ShapeDtypeStruct((batch, SUBLANES, width), F32),
                   jax.ShapeDtypeStruct((batch, 2 * SUBLANES, width), F32),
                   jax.ShapeDtypeStruct((n_s, width), F32)],
        scratch_shapes=[pltpu.VMEM((MIX_TILE + SUBLANES, width), F32),
                        pltpu.VMEM((MIX_TILE + 2 * SUBLANES, width), F32),
                        pltpu.VMEM((1, width), F32)],
        compiler_params=pltpu.CompilerParams(dimension_semantics=("arbitrary", "arbitrary")),
        name="rg_pool",
    )(h_all, h_all, h_all, cbuf_t, h0, pbuf_t, *weights)


def _post_mix_kernel(x_ref, o_hg_ref, o_rg_ref, o_pl_ref, w_ref, g_ref, b_ref, rwh_ref, rwl_ref, rb_ref,
                     x1_ref, idx_ref, gate_ref, *, alpha, n_experts):
    n_hg = o_hg_ref.shape[1]
    n_rg = o_rg_ref.shape[1]
    y = _dot(o_hg_ref[...], w_ref[0:n_hg, :])
    y = y + _dot(o_rg_ref[...], w_ref[n_hg:n_hg + n_rg, :])
    y = y + _dot(o_pl_ref[...], w_ref[n_hg + n_rg:, :])
    x1 = _layer_norm(alpha * x_ref[...] + y, g_ref[...], b_ref[...])
    x1_ref[...] = x1

    x_hi = x1.astype(BF16)
    x_lo = (x1 - x_hi.astype(F32)).astype(BF16)
    logits = (jnp.dot(x_hi, rwh_ref[...], preferred_element_type=F32)
              + jnp.dot(x_lo, rwh_ref[...], preferred_element_type=F32)
              + jnp.dot(x_hi, rwl_ref[...], preferred_element_type=F32)) + rb_ref[...]

    lane = lax.broadcasted_iota(I32, logits.shape, 1).astype(F32)
    vals = jnp.where(lane < n_experts, logits, NEG_BIG)
    idx_out = jnp.zeros(logits.shape, F32)
    exp_out = jnp.zeros(logits.shape, F32)
    v0 = None
    denom = None
    for k in range(TOP_K):
        m = jnp.max(vals, axis=-1, keepdims=True)
        idx = jnp.min(jnp.where(vals == m, lane, float(LANES)), axis=-1, keepdims=True)
        if k == 0:
            v0 = m
        e = jnp.exp(m - v0)
        denom = e if denom is None else denom + e
        idx_out = jnp.where(lane == k, idx, idx_out)
        exp_out = jnp.where(lane == k, e, exp_out)
        vals = jnp.where(lane == idx, 2.0 * NEG_BIG, vals)
    idx_ref[...] = idx_out.astype(I32)
    gate_ref[...] = exp_out / denom


def _post_mix(x, o_hg, o_rg, o_pl, w_out_bf16, layer, ln_g, ln_b, rw_hi, rw_lo, rb_pad, alpha, n_experts):
    m, d = x.shape

    def rows(arr):
        return pl.BlockSpec((POST_TM, arr.shape[1]), lambda i: (i, 0))

    def full(arr):
        return pl.BlockSpec(arr.shape, lambda i: (0,) * arr.ndim)

    w_spec = pl.BlockSpec((None,) + w_out_bf16.shape[1:], lambda i: (layer, 0, 0))
    return pl.pallas_call(
        functools.partial(_post_mix_kernel, alpha=alpha, n_experts=n_experts),
        grid=(m // POST_TM,),
        in_specs=[rows(x), rows(o_hg), rows(o_rg), rows(o_pl), w_spec, full(ln_g), full(ln_b),
                  full(rw_hi), full(rw_lo), full(rb_pad)],
        out_specs=[pl.BlockSpec((POST_TM, d), lambda i: (i, 0)),
                   pl.BlockSpec((POST_TM, LANES), lambda i: (i, 0)),
                   pl.BlockSpec((POST_TM, LANES), lambda i: (i, 0))],
        out_shape=[jax.ShapeDtypeStruct((m, d), F32), jax.ShapeDtypeStruct((m, LANES), I32),
                   jax.ShapeDtypeStruct((m, LANES), F32)],
        compiler_params=pltpu.CompilerParams(dimension_semantics=("parallel",),
                                             vmem_limit_bytes=VMEM_BYTES_V7X * 3 // 4),
        name="post_mix",
    )(x, o_hg, o_rg, o_pl, w_out_bf16, ln_g, ln_b, rw_hi, rw_lo, rb_pad)


def _for_rows(lo, hi, fn):
    g_lo = lo >> SUBLANE_SHIFT
    g_hi = hi >> SUBLANE_SHIFT

    def group_body(g, carry):
        for u in range(SUBLANES):
            fn(g, u)
        return carry

    def single_body(r, carry):
        fn(g_hi, r - (g_hi << SUBLANE_SHIFT))
        return carry

    lax.fori_loop(g_lo, g_hi, group_body, 0)
    lax.fori_loop(g_hi << SUBLANE_SHIFT, hi, single_body, 0)


def _moe_kernel(item_expert, item_rows, item_start, order_hbm, x_hbm, wgu_ref, bgu_ref, wd_ref, bd_ref, y_hbm,
                acc_ref, stage_ref, xg_ref, idx_ref, ctr_ref, sem_idx, sem_in, sem_out, *, n_tokens, n_chunks):
    del item_expert
    w = pl.program_id(0)
    c = pl.program_id(1)
    n_items = pl.num_programs(0)
    last_c = n_chunks - 1
    n = item_rows[w]
    n_blk = (n + MOE_BLK - 1) // MOE_BLK
    d = xg_ref.shape[1]
    tn = wgu_ref.shape[1]
    blk_groups = MOE_BLK // SUBLANES
    nxt = jnp.minimum(w + 1, n_items - 1)
    n_next = jnp.where(w + 1 < n_items, item_rows[nxt], 0)
    this_win = pl.multiple_of((w & 1) * MOE_IDX_LEN, SMEM_1D_TILE)
    next_win = pl.multiple_of((1 - (w & 1)) * MOE_IDX_LEN, SMEM_1D_TILE)

    def window_offset(item):
        return item_start[item] & (SMEM_1D_TILE - 1)

    def window_copy(item, win):
        src0 = pl.multiple_of(item_start[item] - window_offset(item), SMEM_1D_TILE)
        return pltpu.make_async_copy(order_hbm.at[pl.ds(src0, MOE_IDX_LEN)], idx_ref.at[pl.ds(win, MOE_IDX_LEN)],
                                     sem_idx)

    def gather_row(first, last_row, g, u):
        row = jnp.minimum((g << SUBLANE_SHIFT) + u, last_row)
        pair = idx_ref[first + row]
        pltpu.make_async_copy(x_hbm.at[pl.ds(pair >> TOP_K_SHIFT, 1)], stage_ref.at[g, pl.ds(u, 1)],
                              sem_in).start()

    def start_gather(item, win, lo, hi):
        first = win + window_offset(item)
        last_row = jnp.maximum(item_rows[item] - 1, 0)
        _for_rows(lo, hi, functools.partial(gather_row, first, last_row))

    def start_scatter(lo, hi):
        first = this_win + window_offset(w)

        def issue(g, u):
            pair = idx_ref[first + (g << SUBLANE_SHIFT) + u]
            dst = (pair & (TOP_K - 1)) * n_tokens + (pair >> TOP_K_SHIFT)
            pltpu.make_async_copy(acc_ref.at[g, pl.ds(u, 1)], y_hbm.at[pl.ds(dst, 1)], sem_out).start()
        _for_rows(lo, hi, issue)

    def wait_gather(count):
        _for_rows(0, count, lambda g, u: pltpu.make_async_copy(
            x_hbm.at[pl.ds(0, 1)], stage_ref.at[0, pl.ds(0, 1)], sem_in).wait())

    def wait_scatter(count):
        _for_rows(0, count, lambda g, u: pltpu.make_async_copy(
            acc_ref.at[0, pl.ds(0, 1)], y_hbm.at[pl.ds(0, 1)], sem_out).wait())

    @pl.when(jnp.logical_and(w == 0, c == 0))
    def _():
        stage_ref[...] = jnp.zeros_like(stage_ref)

        @pl.when(n > 0)
        def _():
            window_copy(0, this_win).start()
            window_copy(0, this_win).wait()
            start_gather(0, this_win, 0, n)
            ctr_ref[1] = n

    @pl.when(n > 0)
    def _():
        @pl.when(c == 0)
        def _():
            window_copy(nxt, next_win).start()
            wait_gather(ctr_ref[1])

            @pl.when(w > 0)
            def _():
                wait_scatter(item_rows[jnp.maximum(w - 1, 0)])

            def cast_body(i, carry):
                gs = pl.ds(pl.multiple_of(i * blk_groups, blk_groups), blk_groups)
                rs = pl.ds(pl.multiple_of(i * MOE_BLK, MOE_BLK), MOE_BLK)
                xg_ref[rs, :] = stage_ref[gs].reshape(MOE_BLK, d).astype(BF16)
                acc_ref[gs] = jnp.broadcast_to(bd_ref[...], (blk_groups, SUBLANES, d))
                return carry
            lax.fori_loop(0, n_blk, cast_body, 0)
            window_copy(nxt, next_win).wait()
            ctr_ref[0] = 0

        next_first = next_win + window_offset(nxt)
        next_last_row = jnp.maximum(n_next - 1, 0)

        sel_r = lax.broadcasted_iota(I32, (tn, tn // 2), 0)
        sel_c = lax.broadcasted_iota(I32, (tn, tn // 2), 1)
        pick = jnp.where(sel_r == 2 * sel_c, 1.0, 0.0).astype(BF16)

        def sub_tile(row0, rows):
            groups = rows // SUBLANES
            gs = pl.ds(pl.multiple_of(row0 >> SUBLANE_SHIFT, blk_groups), groups)
            h = jnp.dot(xg_ref[pl.ds(row0, rows), :], wgu_ref[...].astype(BF16),
                        preferred_element_type=F32) + bgu_ref[...]
            h_next = pltpu.roll(h, tn - 1, axis=1)
            gate = jnp.minimum(h, SWIGLU_LIMIT)
            up = jnp.clip(h_next, -SWIGLU_LIMIT, SWIGLU_LIMIT)
            act = (up + 1.0) * gate * _sigmoid(gate * SWIGLU_ALPHA)
            act_dense = jnp.dot(act.astype(BF16), pick, preferred_element_type=F32)
            upd = jnp.dot(act_dense.astype(BF16), wd_ref[...].astype(BF16), preferred_element_type=F32)
            acc_ref[gs] = acc_ref[gs] + upd.reshape(groups, SUBLANES, d)

            issued = ctr_ref[0]
            g_base = issued >> SUBLANE_SHIFT
            for k in range(rows // n_chunks):
                gather_row(next_first, next_last_row, g_base + k // SUBLANES, k % SUBLANES)
            ctr_ref[0] = issued + rows // n_chunks

            @pl.when(c == last_c)
            def _():
                start_scatter(row0, jnp.minimum(row0 + rows, n))

        big = MOE_SUB_ROWS[0]
        big_blocks = big // MOE_BLK
        left = n_blk & (big_blocks - 1)
        merge = jnp.logical_and(left == 1, n_blk > big_blocks)
        n_big = n_blk // big_blocks - merge.astype(I32)

        def big_body(i, carry):
            sub_tile(pl.multiple_of(i * big, big), big)
            return carry
        lax.fori_loop(0, n_big, big_body, 0)
        row = n_big * big

        @pl.when(merge)
        def _(row=row):
            sub_tile(pl.multiple_of(row, MOE_BLK), big + MOE_BLK)
        row = row + jnp.where(merge, big + MOE_BLK, 0)
        left = jnp.where(merge, 0, left)
        for rows in MOE_SUB_ROWS[1:]:
            take = (left & (rows // MOE_BLK)) != 0

            @pl.when(take)
            def _(row=row, rows=rows):
                sub_tile(pl.multiple_of(row, MOE_BLK), rows)
            row = row + jnp.where(take, rows, 0)

        @pl.when(c == last_c)
        def _():
            issued = ctr_ref[0]

            @pl.when(issued < n_next)
            def _():
                start_gather(nxt, next_win, issued, n_next)
            total = jnp.maximum(issued, n_next)
            ctr_ref[1] = total

            @pl.when(n_next == 0)
            def _():
                wait_scatter(n)
                wait_gather(total)


def _moe(x1, tables, wgu, bgu, wd, bd, layer, n_items):
    m, d = x1.shape
    depth, n_exp, _, n_gu = wgu.shape
    n_chunks = n_gu // MOE_TN
    half = MOE_TN // 2
    row_groups = MOE_TM // SUBLANES

    def chunk(c, rows_ref, w):
        return jnp.where(rows_ref[w] > 0, c, n_chunks - 1)

    grid_spec = pltpu.PrefetchScalarGridSpec(
        num_scalar_prefetch=3,
        grid=(n_items, n_chunks),
        in_specs=[
            pl.BlockSpec(memory_space=pl.ANY),
            pl.BlockSpec(memory_space=pl.ANY),
            pl.BlockSpec((None, None, d, MOE_TN), lambda w, c, ie, ir, ist: (layer, ie[w], 0, chunk(c, ir, w))),
            pl.BlockSpec((None, None, 1, MOE_TN), lambda w, c, ie, ir, ist: (layer, ie[w], 0, chunk(c, ir, w))),
            pl.BlockSpec((None, None, half, d), lambda w, c, ie, ir, ist: (layer, ie[w], chunk(c, ir, w), 0)),
            pl.BlockSpec((None, None, 1, d), lambda w, c, ie, ir, ist: (layer, ie[w], 0, 0)),
        ],
        out_specs=pl.BlockSpec(memory_space=pl.ANY),
        scratch_shapes=[pltpu.VMEM((row_groups, SUBLANES, d), F32), pltpu.VMEM((row_groups, SUBLANES, d), F32),
                        pltpu.VMEM((MOE_TM, d), BF16),
                        pltpu.SMEM((2 * MOE_IDX_LEN,), I32), pltpu.SMEM((2,), I32),
                        pltpu.SemaphoreType.DMA, pltpu.SemaphoreType.DMA, pltpu.SemaphoreType.DMA],
    )
    assert all((rows // n_chunks) % SUBLANES == 0 for rows in MOE_SUB_ROWS + (MOE_SUB_ROWS[0] + MOE_BLK,))
    return pl.pallas_call(
        functools.partial(_moe_kernel, n_tokens=m, n_chunks=n_chunks),
        grid_spec=grid_spec,
        out_shape=jax.ShapeDtypeStruct((TOP_K * m, d), F32),
        compiler_params=pltpu.CompilerParams(dimension_semantics=("arbitrary", "arbitrary"),
                                             vmem_limit_bytes=VMEM_BYTES_V7X * 7 // 8),
        name="moe_experts",
    )(tables["expert"], tables["rows"], tables["start"], tables["order"], x1,
      wgu, bgu.reshape(depth, n_exp, 1, n_gu), wd, bd.reshape(depth, n_exp, 1, d))


def _route_tables(top_idx, n_experts, n_items):
    m = top_idx.shape[0]
    n_pairs = m * TOP_K
    pair_expert = top_idx.reshape(-1)
    order = jnp.argsort(pair_expert, stable=True).astype(I32)
    counts = jnp.sum((pair_expert[:, None] == jnp.arange(n_experts, dtype=I32)[None, :]).astype(I32), axis=0)
    row_end = jnp.cumsum(counts)
    row_off = row_end - counts
    tiles = (counts + MOE_TM - 1) // MOE_TM
    tile_end = jnp.cumsum(tiles)
    tile_off = tile_end - tiles
    total = tile_end[-1]
    w = jnp.arange(n_items, dtype=I32)
    e_w = jnp.minimum(jnp.searchsorted(tile_end, w, side="right").astype(I32), n_experts - 1)
    j = w - tile_off[e_w]
    active = w < total
    start = jnp.where(active, row_off[e_w] + j * MOE_TM, 0)
    rows = jnp.where(active, jnp.clip(counts[e_w] - j * MOE_TM, 0, MOE_TM), 0)
    e_w = jnp.where(active, e_w, e_w[jnp.maximum(total - 1, 0)])
    padded = -(-n_pairs // SMEM_1D_TILE) * SMEM_1D_TILE + MOE_IDX_LEN
    return {"expert": e_w.astype(I32), "rows": rows.astype(I32), "start": start.astype(I32),
            "order": jnp.pad(order, (0, padded - n_pairs))}


def _combine_kernel(x1_ref, gate_ref, y_ref, g_ref, b_ref, o_ref, *, alpha):
    gates = gate_ref[...]
    acc = alpha * x1_ref[...]
    for k in range(TOP_K):
        acc = acc + gates[:, k:k + 1] * y_ref[k]
    o_ref[...] = _layer_norm(acc, g_ref[...], b_ref[...])


def _combine(x1, gates, y, ln_g, ln_b, alpha):
    m, d = x1.shape
    return pl.pallas_call(
        functools.partial(_combine_kernel, alpha=alpha),
        grid=(m // COMBINE_TM,),
        in_specs=[pl.BlockSpec((COMBINE_TM, d), lambda i: (i, 0)),
                  pl.BlockSpec((COMBINE_TM, LANES), lambda i: (i, 0)),
                  pl.BlockSpec((TOP_K, COMBINE_TM, d), lambda i: (0, i, 0)),
                  pl.BlockSpec((1, d), lambda i: (0, 0)),
                  pl.BlockSpec((1, d), lambda i: (0, 0))],
        out_specs=pl.BlockSpec((COMBINE_TM, d), lambda i: (i, 0)),
        out_shape=jax.ShapeDtypeStruct((m, d), F32),
        compiler_params=pltpu.CompilerParams(dimension_semantics=("parallel",),
                                             vmem_limit_bytes=VMEM_BYTES_V7X * 3 // 4),
        name="moe_combine",
    )(x1, gates, y.reshape(TOP_K, m, d), ln_g, ln_b)


def _block_diag(w):
    nb, n, _ = w.shape
    eye = jnp.eye(nb, dtype=w.dtype)
    return (eye[:, None, :, None] * w[:, :, None, :]).reshape(nb * n, nb * n)


def _row(v):
    return v.reshape(1, -1).astype(F32)


def kernel(x_prompt, x_sample, state_hgrn, state_rglru_h, state_rglru_conv, state_pool, w_in, hg_lb_logits,
           hg_norm_w, rg_conv_w, rg_conv_b, rg_wa, rg_ba, rg_wx, rg_bx, rg_lambda, pool_w, pool_scale, w_out,
           ln1_g, ln1_b, router_w, router_b, moe_w_gate_up, moe_b_gate_up, moe_w_down, moe_b_down, ln2_g, ln2_b):
    depth = w_in.shape[0]
    bp, lp, d_model = x_prompt.shape
    bs = x_sample.shape[0]
    n_p = bp * lp
    m = n_p + bs
    hg_width = hg_lb_logits.shape[1]
    rg_width = rg_lambda.shape[1]
    n_experts = router_w.shape[2]
    n_conv = state_rglru_conv.shape[2]
    n_pool = state_pool.shape[2]
    alpha = float((2 * depth) ** 0.25)
    n_items = n_experts + (m * TOP_K) // MOE_TM
    rg_col0 = 4 * hg_width

    p_lb = jax.nn.softmax(hg_lb_logits.astype(F32), axis=0)
    lbs = jnp.cumsum(p_lb, axis=0)
    lbs = lbs - lbs[0]

    w_in_bf16 = w_in.astype(BF16)
    w_out_bf16 = w_out.astype(BF16)
    x = jnp.concatenate([x_prompt.reshape(n_p, d_model), x_sample.reshape(bs, d_model)], axis=0)
    outs = {k: [] for k in ("hg_p", "hg_s", "h_p", "h_s", "c_p", "c_s", "pl_p", "pl_s")}
    for l in range(depth):
        lb = lbs[l]
        lb_rows = jnp.stack([jnp.log(lb), jnp.log1p(-lb), 1.0 - lb], axis=0)
        rg_w = {"conv_w": rg_conv_w[l], "conv_b": _row(rg_conv_b[l]), "wa": _block_diag(rg_wa[l]).astype(BF16),
                "ba": _row(rg_ba[l]), "wx": _block_diag(rg_wx[l]).astype(BF16), "bx": _row(rg_bx[l]),
                "lam": _row(rg_lambda[l])}
        pl_w = {"w": _block_diag(pool_w[l]).astype(BF16), "scale": _row(pool_scale[l])}

        h_all = _in_proj(x, w_in_bf16, l)
        o_hg, s_p, s_s = _hgrn(h_all, lb_rows, _row(hg_norm_w[l]), state_hgrn, l, bp, lp)
        o_rg, o_pl, h_p, c_tail, p_tail, h_s = _rg_pool(
            h_all, jnp.swapaxes(state_rglru_conv[l], 0, 1), state_rglru_h[l], jnp.swapaxes(state_pool[l], 0, 1),
            rg_w, pl_w, bp, lp, rg_col0)

        rx_s = h_all[n_p:, rg_col0:rg_col0 + rg_width]
        pin_s = h_all[n_p:, rg_col0 + 2 * rg_width:]
        outs["hg_p"].append(s_p)
        outs["hg_s"].append(s_s)
        outs["h_p"].append(h_p.reshape(bp, rg_width))
        outs["h_s"].append(h_s)
        outs["c_p"].append(c_tail[:, SUBLANES - n_conv:])
        outs["c_s"].append(jnp.concatenate([state_rglru_conv[l][:, 1:], rx_s[:, None, :]], axis=1))
        outs["pl_p"].append(p_tail[:, 2 * SUBLANES - n_pool:])
        outs["pl_s"].append(jnp.concatenate([state_pool[l][:, 1:], pin_s[:, None, :]], axis=1))

        rw = jnp.pad(router_w[l].astype(F32), ((0, 0), (0, LANES - n_experts)))
        rw_hi = rw.astype(BF16)
        rw_lo = (rw - rw_hi.astype(F32)).astype(BF16)
        rb_pad = jnp.pad(router_b[l].astype(F32), (0, LANES - n_experts)).reshape(1, LANES)
        x1, top_idx, gates = _post_mix(x, o_hg, o_rg, o_pl, w_out_bf16, l, _row(ln1_g[l]), _row(ln1_b[l]),
                                       rw_hi, rw_lo, rb_pad, alpha, n_experts)

        tables = _route_tables(top_idx[:, :TOP_K], n_experts, n_items)
        y = _moe(x1, tables, moe_w_gate_up, moe_b_gate_up, moe_w_down, moe_b_down, l, n_items)
        x = _combine(x1, gates, y, _row(ln2_g[l]), _row(ln2_b[l]), alpha)

    return (x[:n_p].reshape(bp, lp, d_model), x[n_p:].reshape(bs, 1, d_model),
            jnp.stack(outs["hg_p"]), jnp.stack(outs["hg_s"]), jnp.stack(outs["h_p"]), jnp.stack(outs["h_s"]),
            jnp.stack(outs["c_p"]), jnp.stack(outs["c_s"]), jnp.stack(outs["pl_p"]), jnp.stack(outs["pl_s"]))
```
